```python
import math
import jax, jax.numpy as jnp
from jax import lax
import numpy as np

D_MODEL = 1024
BATCH = 16
SEQ = 2048
DEPTH = 1

HEAD_DIM = 64
HEADS_PER_GROUP = 4
DILATION_GROUPS = ((128, 1), (512, 4), (2048, 16))
N_ATT_GROUPS = len(DILATION_GROUPS)
N_ATT_HEADS = N_ATT_GROUPS * HEADS_PER_GROUP
ATT_WIDTH = N_ATT_HEADS * HEAD_DIM
ATT_OUT_WIDTH = HEADS_PER_GROUP * HEAD_DIM
BLOCK = 128
POOL_WINDOWS = (2, 4, 8, 16)
N_POOL_GROUPS = len(POOL_WINDOWS)
POOL_WIDTH = D_MODEL
POOL_GROUP_WIDTH = POOL_WIDTH // N_POOL_GROUPS
IN_WIDTH = 3 * ATT_WIDTH + POOL_WIDTH + 2 * D_MODEL
D_FF = -(-8 * D_MODEL // (3 * 256)) * 256
N_ADA = 6
DEEPNORM_ALPHA = (2.0 * DEPTH) ** 0.25
DEEPNORM_BETA = (8.0 * DEPTH) ** -0.25
LN_EPS = 1e-5

kernel_name = "hybrid_dilated_attn_pool_deepnorm_adaln"


def alibi_slopes(n):
    def pow2_slopes(m):
        start = 2.0 ** (-8.0 / m)
        return [start ** (i + 1) for i in range(m)]
    if math.log2(n).is_integer():
        s = pow2_slopes(n)
    else:
        c = 2 ** math.floor(math.log2(n))
        s = pow2_slopes(c) + pow2_slopes(2 * c)[0::2][: n - c]
    return jnp.asarray(np.array(sorted(s, reverse=True), dtype=np.float32))


def layer_norm(x, g, b):
    xf = x.astype(jnp.float32)
    mu = jnp.mean(xf, axis=-1, keepdims=True)
    var = jnp.mean(jnp.square(xf - mu), axis=-1, keepdims=True)
    y = (xf - mu) * lax.rsqrt(var + LN_EPS) * g.astype(jnp.float32) + b.astype(jnp.float32)
    return y.astype(x.dtype)


def dilated_window_attention(q, k, v, window, dilation, slopes):
    B, S, H, Dh = q.shape
    steps = window // dilation
    chunk = dilation * BLOCK
    s_pad = -(-S // chunk) * chunk
    L = s_pad // dilation
    nb = L // BLOCK

    def to_blocks(t):
        t = jnp.pad(t, ((0, 0), (0, s_pad - S), (0, 0), (0, 0)))
        t = t.reshape(B, L, dilation, H, Dh).transpose(0, 2, 3, 1, 4)
        return t.reshape(B, dilation, H, nb, BLOCK, Dh)

    def with_prev(t):
        prev = jnp.pad(t[:, :, :, :-1], ((0, 0), (0, 0), (0, 0), (1, 0), (0, 0), (0, 0)))
        return jnp.concatenate([prev, t], axis=4)

    qb = to_blocks(q)
    kw = with_prev(to_blocks(k))
    vw = with_prev(to_blocks(v))

    scores = jnp.einsum('brhnqd,brhnkd->brhnqk', qb, kw).astype(jnp.float32) * (1.0 / math.sqrt(Dh))
    qi = jnp.arange(BLOCK) + BLOCK
    kj = jnp.arange(2 * BLOCK)
    diff = qi[:, None] - kj[None, :]
    key_sub = jnp.arange(nb)[:, None, None] * BLOCK - BLOCK + kj[None, None, :]
    valid = (diff >= 0)[None] & (diff <= steps)[None] & (key_sub >= 0)
    bias = -slopes.astype(jnp.float32)[:, None, None] * (diff * dilation).astype(jnp.float32)[None]
    scores = scores + bias[None, None, :, None]
    scores = jnp.where(valid[None, None, None], scores, -jnp.inf)
    m = jnp.max(scores, axis=-1, keepdims=True)
    p = jnp.exp(scores - m)
    denom = jnp.sum(p, axis=-1, keepdims=True)
    out = jnp.einsum('brhnqk,brhnkd->brhnqd', p, vw.astype(jnp.float32)) / denom
    lse = (m + jnp.log(denom))[..., 0]

    out = out.reshape(B, dilation, H, L, Dh).transpose(0, 3, 1, 2, 4).reshape(B, s_pad, H, Dh)[:, :S]
    lse = lse.reshape(B, dilation, H, L).transpose(0, 3, 1, 2).reshape(B, s_pad, H)[:, :S]
    return out, lse


def causal_multiscale_pool(u):
    B, S, G, C = u.shape
    uf = u.astype(jnp.float32)
    cs = jnp.cumsum(uf, axis=1)
    pos = jnp.arange(S)
    outs = []
    for g, w in enumerate(POOL_WINDOWS):
        csg = cs[:, :, g]
        lagged = jnp.pad(csg, ((0, 0), (w, 0), (0, 0)))[:, :S]
        count = jnp.minimum(pos + 1, w).astype(jnp.float32)[None, :, None]
        outs.append((csg - lagged) / count)
    return jnp.stack(outs, axis=2) - uf


def setup_inputs(seed: int = 0) -> dict:
    key = jax.random.key(seed)
    ks = jax.random.split(key, 20)
    f32 = jnp.float32
    nrm = lambda k, shape, s: jax.random.normal(k, shape, f32) * s
    w_in = nrm(ks[3], (DEPTH, D_MODEL, IN_WIDTH), D_MODEL ** -0.5)
    w_in = w_in.at[:, :, 2 * ATT_WIDTH:3 * ATT_WIDTH].multiply(DEEPNORM_BETA)
    return {
        "x": nrm(ks[0], (BATCH, SEQ, D_MODEL), 1.0),
        "c": nrm(ks[1], (BATCH, D_MODEL), 1.0),
        "w_ada": nrm(ks[2], (DEPTH, D_MODEL, N_ADA * D_MODEL), 0.5 * D_MODEL ** -0.5),
        "b_ada": nrm(ks[4], (DEPTH, N_ADA * D_MODEL), 0.02),
        "w_in": w_in,
        "w_branch_att": nrm(ks[5], (DEPTH, ATT_OUT_WIDTH, D_MODEL), ATT_OUT_WIDTH ** -0.5),
        "w_pool_group": nrm(ks[6], (DEPTH, N_POOL_GROUPS, POOL_GROUP_WIDTH, POOL_GROUP_WIDTH), POOL_GROUP_WIDTH ** -0.5),
        "pool_scale": 1.0 + nrm(ks[7], (DEPTH, POOL_WIDTH), 0.1),
        "w_branch_pool": nrm(ks[8], (DEPTH, POOL_WIDTH, D_MODEL), POOL_WIDTH ** -0.5),
        "w_out": nrm(ks[9], (DEPTH, D_MODEL, D_MODEL), DEEPNORM_BETA * D_MODEL ** -0.5),
        "ln1_g": 1.0 + nrm(ks[10], (DEPTH, D_MODEL), 0.02),
        "ln1_b": nrm(ks[11], (DEPTH, D_MODEL), 0.02),
        "w_gate": nrm(ks[12], (DEPTH, D_MODEL, D_FF), D_MODEL ** -0.5),
        "w_up": nrm(ks[13], (DEPTH, D_MODEL, D_FF), D_MODEL ** -0.5),
        "w_down": nrm(ks[14], (DEPTH, D_FF, D_MODEL), DEEPNORM_BETA * D_FF ** -0.5),
        "ln2_g": 1.0 + nrm(ks[15], (DEPTH, D_MODEL), 0.02),
        "ln2_b": nrm(ks[16], (DEPTH, D_MODEL), 0.02),
    }


def reference(x, c, w_ada, b_ada, w_in, w_branch_att, w_pool_group, pool_scale, w_branch_pool,
              w_out, ln1_g, ln1_b, w_gate, w_up, w_down, ln2_g, ln2_b):
    B, S, D = x.shape
    slopes = alibi_slopes(N_ATT_HEADS).reshape(N_ATT_GROUPS, HEADS_PER_GROUP)
    split_at = [ATT_WIDTH, 2 * ATT_WIDTH, 3 * ATT_WIDTH, 3 * ATT_WIDTH + POOL_WIDTH,
                3 * ATT_WIDTH + POOL_WIDTH + D_MODEL]
    for l in range(DEPTH):
        mod = (jax.nn.silu(c) @ w_ada[l] + b_ada[l])[:, None, :]
        sh1, sc1, g1, sh2, sc2, g2 = jnp.split(mod, N_ADA, axis=-1)

        h = x * (1.0 + sc1) + sh1
        proj = h @ w_in[l]
        q, k, v, pool_in, ga, gb = jnp.split(proj, split_at, axis=-1)
        q = q.reshape(B, S, N_ATT_GROUPS, HEADS_PER_GROUP, HEAD_DIM)
        k = k.reshape(B, S, N_ATT_GROUPS, HEADS_PER_GROUP, HEAD_DIM)
        v = v.reshape(B, S, N_ATT_GROUPS, HEADS_PER_GROUP, HEAD_DIM)

        outs, lses = [], []
        for g, (window, dilation) in enumerate(DILATION_GROUPS):
            o, s = dilated_window_attention(q[:, :, g], k[:, :, g], v[:, :, g], window, dilation, slopes[g])
            outs.append(o)
            lses.append(s)
        mix_w = jax.nn.softmax(jnp.stack(lses, axis=0), axis=0)
        att = jnp.sum(mix_w[..., None] * jnp.stack(outs, axis=0), axis=0)
        att = att.reshape(B, S, ATT_OUT_WIDTH).astype(x.dtype)
        branch_a = att @ w_branch_att[l]

        u = pool_in.reshape(B, S, N_POOL_GROUPS, POOL_GROUP_WIDTH)
        pm = causal_multiscale_pool(u).astype(x.dtype)
        pg = jnp.einsum('bsgc,gce->bsge', pm, w_pool_group[l]).reshape(B, S, POOL_WIDTH) * pool_scale[l]
        branch_b = pg @ w_branch_pool[l]

        merged = jax.nn.sigmoid(ga) * branch_a + jax.nn.sigmoid(gb) * branch_b
        mixer_out = merged @ w_out[l]
        x = layer_norm(DEEPNORM_ALPHA * x + g1 * mixer_out, ln1_g[l], ln1_b[l])

        h2 = x * (1.0 + sc2) + sh2
        ffn = (jax.nn.silu(h2 @ w_gate[l]) * (h2 @ w_up[l])) @ w_down[l]
        x = layer_norm(DEEPNORM_ALPHA * x + g2 * ffn, ln2_g[l], ln2_b[l])
    return x
```

```python
import functools
import math

import jax
import jax.numpy as jnp
import numpy as np
from jax import lax
from jax.experimental import pallas as pl
from jax.experimental.pallas import tpu as pltpu

D_MODEL = 1024
HEAD_DIM = 64
HEADS_PER_GROUP = 4
DILATION_GROUPS = ((128, 1), (512, 4), (2048, 16))
N_ATT_GROUPS = len(DILATION_GROUPS)
N_ATT_HEADS = N_ATT_GROUPS * HEADS_PER_GROUP
ATT_WIDTH = N_ATT_HEADS * HEAD_DIM
GROUP_WIDTH = HEADS_PER_GROUP * HEAD_DIM
BLOCK = 128
POOL_WINDOWS = (2, 4, 8, 16)
POOL_GROUP_WIDTH = D_MODEL // len(POOL_WINDOWS)
POOL_HALO = 16
N_ADA = 6
DEEPNORM_ALPHA = 2.0 ** 0.25
LN_EPS = 1e-5

LANES_V7X = 128
SCOPED_VMEM_LIMIT_BYTES_V7X = 60000 * 1024

BF16 = jnp.bfloat16
F32 = jnp.float32


def _alibi_slopes(n):
    def pow2_slopes(m):
        start = 2.0 ** (-8.0 / m)
        return [start ** (i + 1) for i in range(m)]
    if math.log2(n).is_integer():
        s = pow2_slopes(n)
    else:
        c = 2 ** math.floor(math.log2(n))
        s = pow2_slopes(c) + pow2_slopes(2 * c)[0::2][: n - c]
    return np.array(sorted(s, reverse=True), dtype=np.float32)


def _bias_tables():
    slopes = _alibi_slopes(N_ATT_HEADS).reshape(N_ATT_GROUPS, HEADS_PER_GROUP)
    qi = np.arange(BLOCK) + BLOCK
    kj = np.arange(2 * BLOCK)
    diff = qi[:, None] - kj[None, :]
    tables = []
    for g, (window, dilation) in enumerate(DILATION_GROUPS):
        steps = window // dilation
        valid = (diff >= 0) & (diff <= steps)
        dist = (diff * dilation).astype(np.float32)
        for h in range(HEADS_PER_GROUP):
            bias = -slopes[g, h] * dist
            normal = np.where(valid, bias, -np.inf).astype(np.float32)
            first = np.where(valid & (kj[None, :] >= BLOCK), bias, -np.inf).astype(np.float32)
            tables += [normal, first]
    return np.stack(tables)


def _layer_norm(y, g, b):
    mu = jnp.mean(y, axis=-1, keepdims=True)
    yc = y - mu
    var = jnp.mean(yc * yc, axis=-1, keepdims=True)
    return yc * lax.rsqrt(var + LN_EPS) * g + b


def _sigmoid(x):
    return 1.0 / (1.0 + jnp.exp(-x))


def _ada_kernel(c_ref, w_ref, b_ref, o_ref):
    c = c_ref[...]
    s = (c * _sigmoid(c)).astype(BF16)
    o_ref[...] = jnp.dot(s, w_ref[...].astype(BF16), preferred_element_type=F32) + b_ref[...]


def _ada(c, w_ada, b_ada):
    batch, d = c.shape
    n = w_ada.shape[1]
    tn = d
    return pl.pallas_call(
        _ada_kernel,
        grid=(n // tn,),
        in_specs=[
            pl.BlockSpec((batch, d), lambda j: (0, 0)),
            pl.BlockSpec((d, tn), lambda j: (0, j)),
            pl.BlockSpec((1, tn), lambda j: (0, j)),
        ],
        out_specs=pl.BlockSpec((batch, tn), lambda j: (0, j)),
        out_shape=jax.ShapeDtypeStruct((batch, n), F32),
        name="ada",
    )(c, w_ada, b_ada.reshape(1, n))


def _inproj_kernel(*refs, seq, row_chunk):
    n_slab = D_MODEL // LANES_V7X
    x_refs = refs[:n_slab]
    mod_ref, w_ref, o_ref, h_ref = refs[n_slab:]
    g = pl.program_id(1)
    n_chunks = seq // BLOCK

    def build(dilation):
        chunks_per_class = n_chunks // dilation

        def body(k, carry):
            r = k // chunks_per_class
            cc = k % chunks_per_class
            src = r + cc * (BLOCK * dilation)
            dst = pl.multiple_of(k * BLOCK, BLOCK)
            for j in range(n_slab):
                lanes = slice(j * LANES_V7X, (j + 1) * LANES_V7X)
                if dilation == 1:
                    xs = x_refs[j][0, pl.ds(dst, BLOCK), :]
                else:
                    xs = x_refs[j][0, pl.ds(src, BLOCK, stride=dilation), :]
                shift = mod_ref[0, :, lanes]
                scale = mod_ref[0, :, D_MODEL + j * LANES_V7X:D_MODEL + (j + 1) * LANES_V7X]
                h_ref[pl.ds(dst, BLOCK), lanes] = (xs * (1.0 + scale) + shift).astype(BF16)
            return carry

        lax.fori_loop(0, n_chunks, body, 0)

    for gi, (_, dilation) in enumerate(DILATION_GROUPS):
        pl.when(g == gi)(functools.partial(build, dilation))

    w = w_ref[0]
    for c in range(seq // row_chunk):
        rows = slice(c * row_chunk, (c + 1) * row_chunk)
        proj = jnp.dot(h_ref[rows, :], w, preferred_element_type=F32)
        q = proj[:, :GROUP_WIDTH] * (1.0 / math.sqrt(HEAD_DIM))
        o_ref[0, 0, rows, :GROUP_WIDTH] = q.astype(BF16)
        o_ref[0, 0, rows, GROUP_WIDTH:] = proj[:, GROUP_WIDTH:].astype(BF16)


def _inproj(x, mod3, w_qkv):
    batch, seq, d = x.shape
    n_slab = d // LANES_V7X
    x_specs = [
        pl.BlockSpec((1, seq, LANES_V7X), functools.partial(lambda b, g, j: (b, 0, j), j=j))
        for j in range(n_slab)
    ]
    return pl.pallas_call(
        functools.partial(_inproj_kernel, seq=seq, row_chunk=512),
        grid=(batch, N_ATT_GROUPS),
        in_specs=x_specs + [
            pl.BlockSpec((1, 1, N_ADA * d), lambda b, g: (b, 0, 0)),
            pl.BlockSpec((1, d, 3 * GROUP_WIDTH), lambda b, g: (g, 0, 0)),
        ],
        out_specs=pl.BlockSpec((1, 1, seq, 3 * GROUP_WIDTH), lambda b, g: (g, b, 0, 0)),
        out_shape=jax.ShapeDtypeStruct((N_ATT_GROUPS, batch, seq, 3 * GROUP_WIDTH), BF16),
        scratch_shapes=[pltpu.VMEM((seq, d), BF16)],
        compiler_params=pltpu.CompilerParams(
            dimension_semantics=("arbitrary", "arbitrary"),
            vmem_limit_bytes=SCOPED_VMEM_LIMIT_BYTES_V7X,
        ),
        name="inproj",
    )(*([x] * n_slab), mod3, w_qkv)


def _attend_kernel(qkv_ref, bias_ref, o_ref, out_s, lse_s, *, seq):
    g = pl.program_id(1)
    n_blocks = seq // BLOCK
    n_pairs = HEADS_PER_GROUP // 2
    lane = lax.broadcasted_iota(jnp.int32, (BLOCK, LANES_V7X), 1)
    low_half = lane < HEAD_DIM
    ones = jnp.ones((2 * BLOCK, LANES_V7X), BF16)

    def group_loop(gi, dilation):
        blocks_per_class = n_blocks // dilation
        has_prev = blocks_per_class > 1

        def body(j, carry):
            row0 = pl.multiple_of(j * BLOCK, BLOCK)
            n = j % blocks_per_class
            r = j // blocks_per_class
            first = (n == 0).astype(jnp.int32)
            prow0 = pl.multiple_of(jnp.maximum(j - 1, 0) * BLOCK, BLOCK)
            nat0 = r + n * (BLOCK * dilation)
            for pair in range(n_pairs):
                c0 = pair * LANES_V7X
                q2 = qkv_ref[0, 0, pl.ds(row0, BLOCK), c0:c0 + LANES_V7X]
                kc = qkv_ref[0, 0, pl.ds(row0, BLOCK), GROUP_WIDTH + c0:GROUP_WIDTH + c0 + LANES_V7X]
                vc = qkv_ref[0, 0, pl.ds(row0, BLOCK), 2 * GROUP_WIDTH + c0:2 * GROUP_WIDTH + c0 + LANES_V7X]
                if has_prev:
                    kp = qkv_ref[0, 0, pl.ds(prow0, BLOCK), GROUP_WIDTH + c0:GROUP_WIDTH + c0 + LANES_V7X]
                    vp = qkv_ref[0, 0, pl.ds(prow0, BLOCK), 2 * GROUP_WIDTH + c0:2 * GROUP_WIDTH + c0 + LANES_V7X]
                    k2 = jnp.concatenate([kp, kc], axis=0)
                    v2 = jnp.concatenate([vp, vc], axis=0)
                    v2e = jnp.concatenate([v2, ones], axis=1)
                else:
                    k2 = kc
                    v2e = jnp.concatenate([vc, ones[:BLOCK]], axis=1)
                accs, dens, maxes = [], [], []
                for hh in range(2):
                    head = pair * 2 + hh
                    keep = low_half if hh == 0 else jnp.logical_not(low_half)
                    qm = jnp.where(keep, q2, jnp.zeros_like(q2))
                    s = lax.dot_general(qm, k2, (((1,), (1,)), ((), ())), preferred_element_type=F32)
                    table = (gi * HEADS_PER_GROUP + head) * 2
                    if has_prev:
                        s = s + bias_ref[table + first]
                    else:
                        s = s + bias_ref[table, :, BLOCK:]
                    m = jnp.max(s, axis=-1, keepdims=True)
                    p = jnp.exp(s - m).astype(BF16)
                    o = jnp.dot(p, v2e, preferred_element_type=F32)
                    accs.append(o[:, :LANES_V7X])
                    dens.append(o[:, LANES_V7X:])
                    maxes.append(jnp.broadcast_to(m, (BLOCK, LANES_V7X)))
                acc = jnp.where(low_half, accs[0], accs[1])
                den = jnp.where(low_half, dens[0], dens[1])
                mx = jnp.where(low_half, maxes[0], maxes[1])
                out = acc / den
                lse = mx + jnp.log(den)
                if dilation == 1:
                    out_s[gi, pair, pl.ds(row0, BLOCK), :] = out
                    lse_s[gi, pair, pl.ds(row0, BLOCK), :] = lse
                else:
                    out_s[gi, pair, pl.ds(nat0, BLOCK, stride=dilation), :] = out
                    lse_s[gi, pair, pl.ds(nat0, BLOCK, stride=dilation), :] = lse
            return carry

        lax.fori_loop(0, n_blocks, body, 0)

    for gi, (_, dilation) in enumerate(DILATION_GROUPS):
        pl.when(g == gi)(functools.partial(group_loop, gi, dilation))

    @pl.when(g == N_ATT_GROUPS - 1)
    def _merge():
        rows_per_step = 256
        for pair in range(n_pairs):
            for c in range(seq // rows_per_step):
                rows = slice(c * rows_per_step, (c + 1) * rows_per_step)
                lses = [lse_s[gi, pair, rows, :] for gi in range(N_ATT_GROUPS)]
                top = functools.reduce(jnp.maximum, lses)
                es = [jnp.exp(l - top) for l in lses]
                den = functools.reduce(lambda a, b: a + b, es)
                num = functools.reduce(
                    lambda a, b: a + b, [es[gi] * out_s[gi, pair, rows, :] for gi in range(N_ATT_GROUPS)])
                o_ref[0, rows, pair * LANES_V7X:(pair + 1) * LANES_V7X] = (num / den).astype(BF16)


def _attend(qkv, bias):
    _, batch, seq, width = qkv.shape
    n_pairs = HEADS_PER_GROUP // 2
    return pl.pallas_call(
        functools.partial(_attend_kernel, seq=seq),
        grid=(batch, N_ATT_GROUPS),
        in_specs=[
            pl.BlockSpec((1, 1, seq, width), lambda b, g: (g, b, 0, 0)),
            pl.BlockSpec(bias.shape, lambda b, g: (0, 0, 0)),
        ],
        out_specs=pl.BlockSpec((1, seq, GROUP_WIDTH), lambda b, g: (b, 0, 0)),
        out_shape=jax.ShapeDtypeStruct((batch, seq, GROUP_WIDTH), BF16),
        scratch_shapes=[
            pltpu.VMEM((N_ATT_GROUPS, n_pairs, seq, LANES_V7X), F32),
            pltpu.VMEM((N_ATT_GROUPS, n_pairs, seq, LANES_V7X), F32),
        ],
        compiler_params=pltpu.CompilerParams(
            dimension_semantics=("arbitrary", "arbitrary"),
            vmem_limit_bytes=SCOPED_VMEM_LIMIT_BYTES_V7X,
        ),
        name="attend",
    )(qkv, bias)


def _mixer_kernel(x_ref, xh_ref, mod_ref, att_ref, wp_ref, wpg_ref, ps_ref, wbp_ref, wba_ref, wout_ref,
                  lg_ref, lb_ref, o_ref, u_s, *, tm):
    i = pl.program_id(1)
    d = D_MODEL
    shift = mod_ref[0, :, 0:d]
    scale = mod_ref[0, :, d:2 * d]
    gate_c = mod_ref[0, :, 2 * d:3 * d]
    x = x_ref[0]
    h = (x * (1.0 + scale) + shift).astype(BF16)
    hh = (xh_ref[0] * (1.0 + scale) + shift).astype(BF16)

    u_halo = jnp.dot(hh, wp_ref[:, :d], preferred_element_type=F32)
    u_s[0:POOL_HALO, :] = jnp.where(i > 0, u_halo, jnp.zeros_like(u_halo))
    u_s[POOL_HALO:POOL_HALO + tm, :] = jnp.dot(h, wp_ref[:, :d], preferred_element_type=F32)
    gates = jnp.dot(h, wp_ref[:, d:], preferred_element_type=F32)

    pos = i * tm + lax.broadcasted_iota(jnp.int32, (tm, 1), 0)
    branch_b = None
    for gi, w in enumerate(POOL_WINDOWS):
        cols = slice(gi * POOL_GROUP_WIDTH, (gi + 1) * POOL_GROUP_WIDTH)
        u = u_s[POOL_HALO:POOL_HALO + tm, cols]
        total = u
        for j in range(1, w):
            total = total + u_s[POOL_HALO - j:POOL_HALO - j + tm, cols]
        count = jnp.minimum(pos + 1, w).astype(F32)
        pm = total / count - u
        pg = jnp.dot(pm.astype(BF16), wpg_ref[gi], preferred_element_type=F32) * ps_ref[:, cols]
        part = jnp.dot(pg.astype(BF16), wbp_ref[cols, :], preferred_element_type=F32)
        branch_b = part if branch_b is None else branch_b + part

    branch_a = jnp.dot(att_ref[0], wba_ref[...], preferred_element_type=F32)
    merged = _sigmoid(gates[:, :d]) * branch_a + _sigmoid(gates[:, d:]) * branch_b
    mixer_out = jnp.dot(merged.astype(BF16), wout_ref[...], preferred_element_type=F32)
    o_ref[0] = _layer_norm(DEEPNORM_ALPHA * x + gate_c * mixer_out, lg_ref[...], lb_ref[...])


def _const_spec(shape):
    return pl.BlockSpec(shape, lambda b, i: (0,) * len(shape))


def _mixer(x, mod3, att, wp, wpg, pool_scale, wbp, wba, wout, ln_g, ln_b, *, tm):
    batch, seq, d = x.shape
    halo_blocks_per_tile = tm // POOL_HALO
    return pl.pallas_call(
        functools.partial(_mixer_kernel, tm=tm),
        grid=(batch, seq // tm),
        in_specs=[
            pl.BlockSpec((1, tm, d), lambda b, i: (b, i, 0)),
            pl.BlockSpec((1, POOL_HALO, d), lambda b, i: (b, jnp.maximum(i * halo_blocks_per_tile - 1, 0), 0)),
            pl.BlockSpec((1, 1, N_ADA * d), lambda b, i: (b, 0, 0)),
            pl.BlockSpec((1, tm, GROUP_WIDTH), lambda b, i: (b, i, 0)),
            _const_spec(wp.shape), _const_spec(wpg.shape), _const_spec(pool_scale.shape),
            _const_spec(wbp.shape), _const_spec(wba.shape), _const_spec(wout.shape),
            _const_spec(ln_g.shape), _const_spec(ln_b.shape),
        ],
        out_specs=pl.BlockSpec((1, tm, d), lambda b, i: (b, i, 0)),
        out_shape=jax.ShapeDtypeStruct((batch, seq, d), F32),
        scratch_shapes=[pltpu.VMEM((POOL_HALO + tm, d), F32)],
        compiler_params=pltpu.CompilerParams(
            dimension_semantics=("arbitrary", "arbitrary"),
            vmem_limit_bytes=SCOPED_VMEM_LIMIT_BYTES_V7X,
        ),
        name="mixer",
    )(x, x, mod3, att, wp, wpg, pool_scale, wbp, wba, wout, ln_g, ln_b)


def _ffn_kernel(x_ref, mod_ref, wg_ref, wu_ref, wd_ref, lg_ref, lb_ref, o_ref):
    d = D_MODEL
    shift = mod_ref[0, :, 3 * d:4 * d]
    scale = mod_ref[0, :, 4 * d:5 * d]
    gate_c = mod_ref[0, :, 5 * d:6 * d]
    x = x_ref[0]
    h = (x * (1.0 + scale) + shift).astype(BF16)
    gt = jnp.dot(h, wg_ref[...], preferred_element_type=F32)
    up = jnp.dot(h, wu_ref[...], preferred_element_type=F32)
    act = (gt * _sigmoid(gt) * up).astype(BF16)
    ffn = jnp.dot(act, wd_ref[...], preferred_element_type=F32)
    o_ref[0] = _layer_norm(DEEPNORM_ALPHA * x + gate_c * ffn, lg_ref[...], lb_ref[...])


def _ffn(x1, mod3, wg, wu, wd, ln_g, ln_b, *, tm):
    batch, seq, d = x1.shape
    return pl.pallas_call(
        _ffn_kernel,
        grid=(batch, seq // tm),
        in_specs=[
            pl.BlockSpec((1, tm, d), lambda b, i: (b, i, 0)),
            pl.BlockSpec((1, 1, N_ADA * d), lambda b, i: (b, 0, 0)),
            _const_spec(wg.shape), _const_spec(wu.shape), _const_spec(wd.shape),
            _const_spec(ln_g.shape), _const_spec(ln_b.shape),
        ],
        out_specs=pl.BlockSpec((1, tm, d), lambda b, i: (b, i, 0)),
        out_shape=jax.ShapeDtypeStruct((batch, seq, d), F32),
        compiler_params=pltpu.CompilerParams(
            dimension_semantics=("arbitrary", "arbitrary"),
            vmem_limit_bytes=SCOPED_VMEM_LIMIT_BYTES_V7X,
        ),
        name="ffn",
    )(x1, mod3, wg, wu, wd, ln_g, ln_b)


def kernel(x, c, w_ada, b_ada, w_in, w_branch_att, w_pool_group, pool_scale, w_branch_pool, w_out, ln1_g, ln1_b,
           w_gate, w_up, w_down, ln2_g, ln2_b):
    batch, seq, d = x.shape
    assert d == D_MODEL and w_ada.shape[0] == 1, "one layer of width D_MODEL"
    assert seq == BLOCK * DILATION_GROUPS[-1][1], "sequence must be one block per residue class of the widest dilation"
    l = 0
    mod = _ada(c, w_ada[l], b_ada[l])
    mod3 = mod.reshape(batch, 1, N_ADA * d)

    w_in_l = w_in[l]
    w_qkv = jnp.stack([
        jnp.concatenate([w_in_l[:, part * ATT_WIDTH + g * GROUP_WIDTH:part * ATT_WIDTH + (g + 1) * GROUP_WIDTH]
                         for part in range(3)], axis=1)
        for g in range(N_ATT_GROUPS)]).astype(BF16)
    qkv = _inproj(x, mod3, w_qkv)
    att = _attend(qkv, jnp.asarray(_bias_tables()))

    x1 = _mixer(
        x, mod3, att,
        w_in_l[:, 3 * ATT_WIDTH:].astype(BF16), w_pool_group[l].astype(BF16), pool_scale[l].reshape(1, d),
        w_branch_pool[l].astype(BF16), w_branch_att[l].astype(BF16), w_out[l].astype(BF16),
        ln1_g[l].reshape(1, d), ln1_b[l].reshape(1, d), tm=512)
    return _ffn(x1, mod3, w_gate[l].astype(BF16), w_up[l].astype(BF16), w_down[l].astype(BF16),
                ln2_g[l].reshape(1, d), ln2_b[l].reshape(1, d), tm=512)
```

```python
import functools
import math

import jax
import jax.numpy as jnp
import numpy as np
from jax import lax
from jax.experimental import pallas as pl
from jax.experimental.pallas import tpu as pltpu

D_MODEL = 1024
HEAD_DIM = 64
HEADS_PER_GROUP = 4
DILATION_GROUPS = ((128, 1), (512, 4), (2048, 16))
N_ATT_GROUPS = len(DILATION_GROUPS)
N_ATT_HEADS = N_ATT_GROUPS * HEADS_PER_GROUP
ATT_WIDTH = N_ATT_HEADS * HEAD_DIM
GROUP_WIDTH = HEADS_PER_GROUP * HEAD_DIM
BLOCK = 128
POOL_WINDOWS = (2, 4, 8, 16)
POOL_GROUP_WIDTH = D_MODEL // len(POOL_WINDOWS)
POOL_HALO = 16
N_ADA = 6
DEEPNORM_ALPHA = 2.0 ** 0.25
LN_EPS = 1e-5

LANES_V7X = 128
SCOPED_VMEM_LIMIT_BYTES_V7X = 60000 * 1024

BF16 = jnp.bfloat16
F32 = jnp.float32


def _alibi_slopes(n):
    def pow2_slopes(m):
        start = 2.0 ** (-8.0 / m)
        return [start ** (i + 1) for i in range(m)]
    if math.log2(n).is_integer():
        s = pow2_slopes(n)
    else:
        c = 2 ** math.floor(math.log2(n))
        s = pow2_slopes(c) + pow2_slopes(2 * c)[0::2][: n - c]
    return np.array(sorted(s, reverse=True), dtype=np.float32)


def _bias_tables():
    slopes = _alibi_slopes(N_ATT_HEADS).reshape(N_ATT_GROUPS, HEADS_PER_GROUP)
    qi = np.arange(BLOCK) + BLOCK
    kj = np.arange(2 * BLOCK)
    diff = qi[:, None] - kj[None, :]
    banded, paired = [], []
    for g, (window, dilation) in enumerate(DILATION_GROUPS):
        steps = window // dilation
        valid = (diff >= 0) & (diff <= steps)
        dist = (diff * dilation).astype(np.float32)
        for h in range(HEADS_PER_GROUP):
            bias = -slopes[g, h] * dist
            normal = np.where(valid, bias, -np.inf).astype(np.float32)
            first = np.where(valid & (kj[None, :] >= BLOCK), bias, -np.inf).astype(np.float32)
            if _blocks_per_class(dilation) > 1:
                banded += [normal, first]
            else:
                own = normal[:, BLOCK:]
                off = np.full_like(own, -np.inf)
                paired.append(np.block([[own, off], [off, own]]))
    return np.stack(banded), np.stack(paired)


def _blocks_per_class(dilation):
    return DILATION_GROUPS[-1][1] // dilation


def _layer_norm(y, g, b):
    mu = jnp.mean(y, axis=-1, keepdims=True)
    yc = y - mu
    var = jnp.mean(yc * yc, axis=-1, keepdims=True)
    return yc * lax.rsqrt(var + LN_EPS) * g + b


def _sigmoid(x):
    return 1.0 / (1.0 + jnp.exp(-x))


def _ada_kernel(c_ref, w_ref, b_ref, o_ref):
    c = c_ref[...]
    s = (c * _sigmoid(c)).astype(BF16)
    o_ref[...] = jnp.dot(s, w_ref[...].astype(BF16), preferred_element_type=F32) + b_ref[...]


def _ada(c, w_ada, b_ada):
    batch, d = c.shape
    n = w_ada.shape[1]
    tn = d
    return pl.pallas_call(
        _ada_kernel,
        grid=(n // tn,),
        in_specs=[
            pl.BlockSpec((batch, d), lambda j: (0, 0)),
            pl.BlockSpec((d, tn), lambda j: (0, j)),
            pl.BlockSpec((1, tn), lambda j: (0, j)),
        ],
        out_specs=pl.BlockSpec((batch, tn), lambda j: (0, j)),
        out_shape=jax.ShapeDtypeStruct((batch, n), F32),
        name="ada",
    )(c, w_ada, b_ada.reshape(1, n))


def _inproj_kernel(*refs, seq, row_chunk):
    n_slab = D_MODEL // LANES_V7X
    x_refs = refs[:n_slab]
    mod_ref, w_ref, o_ref, h_ref = refs[n_slab:]
    g = pl.program_id(1)
    n_blocks = seq // BLOCK
    blocks_per_chunk = row_chunk // BLOCK
    n_row_chunks = seq // row_chunk

    def modulate_rows(dilation, chunk):
        blocks_per_class = n_blocks // dilation
        for k in range(chunk * blocks_per_chunk, (chunk + 1) * blocks_per_chunk):
            r, n = divmod(k, blocks_per_class)
            src = r + n * (BLOCK * dilation)
            for j in range(n_slab):
                lanes = slice(j * LANES_V7X, (j + 1) * LANES_V7X)
                if dilation == 1:
                    xs = x_refs[j][0, src:src + BLOCK, :]
                else:
                    xs = x_refs[j][0, pl.ds(src, BLOCK, stride=dilation), :]
                shift = mod_ref[0, :, lanes]
                scale = mod_ref[0, :, D_MODEL + j * LANES_V7X:D_MODEL + (j + 1) * LANES_V7X]
                h_ref[k * BLOCK:(k + 1) * BLOCK, lanes] = (xs * (1.0 + scale) + shift).astype(BF16)

    def project_rows(chunk):
        rows = slice(chunk * row_chunk, (chunk + 1) * row_chunk)
        proj = jnp.dot(h_ref[rows, :], w_ref[0], preferred_element_type=F32)
        q = proj[:, :GROUP_WIDTH] * (1.0 / math.sqrt(HEAD_DIM))
        o_ref[0, 0, rows, :GROUP_WIDTH] = q.astype(BF16)
        o_ref[0, 0, rows, GROUP_WIDTH:] = proj[:, GROUP_WIDTH:].astype(BF16)

    def run(dilation):
        modulate_rows(dilation, 0)
        for chunk in range(n_row_chunks):
            if chunk + 1 < n_row_chunks:
                modulate_rows(dilation, chunk + 1)
            project_rows(chunk)

    for gi, (_, dilation) in enumerate(DILATION_GROUPS):
        pl.when(g == gi)(functools.partial(run, dilation))


def _inproj(x, mod3, w_qkv):
    batch, seq, d = x.shape
    n_slab = d // LANES_V7X
    x_specs = [
        pl.BlockSpec((1, seq, LANES_V7X), functools.partial(lambda b, g, j: (b, 0, j), j=j))
        for j in range(n_slab)
    ]
    return pl.pallas_call(
        functools.partial(_inproj_kernel, seq=seq, row_chunk=512),
        grid=(batch, N_ATT_GROUPS),
        in_specs=x_specs + [
            pl.BlockSpec((1, 1, N_ADA * d), lambda b, g: (b, 0, 0)),
            pl.BlockSpec((1, d, 3 * GROUP_WIDTH), lambda b, g: (g, 0, 0)),
        ],
        out_specs=pl.BlockSpec((1, 1, seq, 3 * GROUP_WIDTH), lambda b, g: (g, b, 0, 0)),
        out_shape=jax.ShapeDtypeStruct((N_ATT_GROUPS, batch, seq, 3 * GROUP_WIDTH), BF16),
        scratch_shapes=[pltpu.VMEM((seq, d), BF16)],
        compiler_params=pltpu.CompilerParams(
            dimension_semantics=("arbitrary", "arbitrary"),
            vmem_limit_bytes=SCOPED_VMEM_LIMIT_BYTES_V7X,
        ),
        name="inproj",
    )(*([x] * n_slab), mod3, w_qkv)


def _attend_kernel(qkv_ref, banded_ref, paired_ref, o_ref, out_s, lse_s, *, seq):
    g = pl.program_id(1)
    n_blocks = seq // BLOCK
    n_pairs = HEADS_PER_GROUP // 2
    ones = jnp.ones((2 * BLOCK, LANES_V7X), BF16)

    def softmax_pv(q2, k2, v2e, load_bias):
        rows = q2.shape[0]
        low_half = lax.broadcasted_iota(jnp.int32, (rows, LANES_V7X), 1) < HEAD_DIM
        accs, dens, maxes = [], [], []
        for hh in range(2):
            keep = low_half if hh == 0 else jnp.logical_not(low_half)
            qm = jnp.where(keep, q2, jnp.zeros_like(q2))
            s = lax.dot_general(qm, k2, (((1,), (1,)), ((), ())), preferred_element_type=F32)
            s = s + load_bias(hh)
            m = jnp.max(s, axis=-1, keepdims=True)
            p = jnp.exp(s - m).astype(BF16)
            o = jnp.dot(p, v2e, preferred_element_type=F32)
            accs.append(o[:, :LANES_V7X])
            dens.append(o[:, LANES_V7X:])
            maxes.append(jnp.broadcast_to(m, (rows, LANES_V7X)))
        acc = jnp.where(low_half, accs[0], accs[1])
        den = jnp.where(low_half, dens[0], dens[1])
        mx = jnp.where(low_half, maxes[0], maxes[1])
        return acc / den, mx + jnp.log(den)

    def cols(part, pair):
        c0 = part * GROUP_WIDTH + pair * LANES_V7X
        return slice(c0, c0 + LANES_V7X)

    def banded_loop(gi, dilation, table0):
        blocks_per_class = n_blocks // dilation

        def body(j, carry):
            row0 = pl.multiple_of(j * BLOCK, BLOCK)
            n = j % blocks_per_class
            r = j // blocks_per_class
            first = (n == 0).astype(jnp.int32)
            prow0 = pl.multiple_of(jnp.maximum(j - 1, 0) * BLOCK, BLOCK)
            nat0 = r + n * (BLOCK * dilation)
            for pair in range(n_pairs):
                q2 = qkv_ref[0, 0, pl.ds(row0, BLOCK), cols(0, pair)]
                k2 = jnp.concatenate([qkv_ref[0, 0, pl.ds(prow0, BLOCK), cols(1, pair)],
                                      qkv_ref[0, 0, pl.ds(row0, BLOCK), cols(1, pair)]], axis=0)
                v2 = jnp.concatenate([qkv_ref[0, 0, pl.ds(prow0, BLOCK), cols(2, pair)],
                                      qkv_ref[0, 0, pl.ds(row0, BLOCK), cols(2, pair)]], axis=0)
                v2e = jnp.concatenate([v2, ones], axis=1)
                out, lse = softmax_pv(
                    q2, k2, v2e, lambda hh: banded_ref[table0 + (pair * 2 + hh) * 2 + first])
                if dilation == 1:
                    out_s[gi, pair, pl.ds(row0, BLOCK), :] = out
                    lse_s[gi, pair, pl.ds(row0, BLOCK), :] = lse
                else:
                    out_s[gi, pair, pl.ds(nat0, BLOCK, stride=dilation), :] = out
                    lse_s[gi, pair, pl.ds(nat0, BLOCK, stride=dilation), :] = lse
            return carry

        lax.fori_loop(0, n_blocks, body, 0, unroll=4)

    def paired_loop(gi, dilation, table0):
        def body(j, carry):
            row0 = pl.multiple_of(j * (2 * BLOCK), 2 * BLOCK)
            for pair in range(n_pairs):
                q2 = qkv_ref[0, 0, pl.ds(row0, 2 * BLOCK), cols(0, pair)]
                k2 = qkv_ref[0, 0, pl.ds(row0, 2 * BLOCK), cols(1, pair)]
                v2e = jnp.concatenate([qkv_ref[0, 0, pl.ds(row0, 2 * BLOCK), cols(2, pair)], ones], axis=1)
                out, lse = softmax_pv(q2, k2, v2e, lambda hh: paired_ref[table0 + pair * 2 + hh])
                for half in range(2):
                    rows = slice(half * BLOCK, (half + 1) * BLOCK)
                    dst = pl.ds(2 * j + half, BLOCK, stride=dilation)
                    out_s[gi, pair, dst, :] = out[rows]
                    lse_s[gi, pair, dst, :] = lse[rows]
            return carry

        lax.fori_loop(0, n_blocks // 2, body, 0, unroll=2)

    n_banded = n_paired = 0
    for gi, (_, dilation) in enumerate(DILATION_GROUPS):
        if _blocks_per_class(dilation) > 1:
            pl.when(g == gi)(functools.partial(banded_loop, gi, dilation, n_banded))
            n_banded += 2 * HEADS_PER_GROUP
        else:
            pl.when(g == gi)(functools.partial(paired_loop, gi, dilation, n_paired))
            n_paired += HEADS_PER_GROUP

    @pl.when(g == N_ATT_GROUPS - 1)
    def _merge():
        rows_per_step = 256
        for pair in range(n_pairs):
            for c in range(seq // rows_per_step):
                rows = slice(c * rows_per_step, (c + 1) * rows_per_step)
                lses = [lse_s[gi, pair, rows, :] for gi in range(N_ATT_GROUPS)]
                top = functools.reduce(jnp.maximum, lses)
                es = [jnp.exp(l - top) for l in lses]
                den = functools.reduce(lambda a, b: a + b, es)
                num = functools.reduce(
                    lambda a, b: a + b, [es[gi] * out_s[gi, pair, rows, :] for gi in range(N_ATT_GROUPS)])
                o_ref[0, rows, pair * LANES_V7X:(pair + 1) * LANES_V7X] = (num / den).astype(BF16)


def _attend(qkv, banded, paired):
    _, batch, seq, width = qkv.shape
    n_pairs = HEADS_PER_GROUP // 2
    return pl.pallas_call(
        functools.partial(_attend_kernel, seq=seq),
        grid=(batch, N_ATT_GROUPS),
        in_specs=[
            pl.BlockSpec((1, 1, seq, width), lambda b, g: (g, b, 0, 0)),
            pl.BlockSpec(banded.shape, lambda b, g: (0, 0, 0)),
            pl.BlockSpec(paired.shape, lambda b, g: (0, 0, 0)),
        ],
        out_specs=pl.BlockSpec((1, seq, GROUP_WIDTH), lambda b, g: (b, 0, 0)),
        out_shape=jax.ShapeDtypeStruct((batch, seq, GROUP_WIDTH), BF16),
        scratch_shapes=[
            pltpu.VMEM((N_ATT_GROUPS, n_pairs, seq, LANES_V7X), F32),
            pltpu.VMEM((N_ATT_GROUPS, n_pairs, seq, LANES_V7X), F32),
        ],
        compiler_params=pltpu.CompilerParams(
            dimension_semantics=("arbitrary", "arbitrary"),
            vmem_limit_bytes=SCOPED_VMEM_LIMIT_BYTES_V7X,
        ),
        name="attend",
    )(qkv, banded, paired)


def _mixer_kernel(x_ref, xh_ref, mod_ref, att_ref, wp_ref, wpg_ref, ps_ref, wbp_ref, wba_ref, wout_ref,
                  lg_ref, lb_ref, o_ref, u_s, *, tm):
    i = pl.program_id(1)
    d = D_MODEL
    shift = mod_ref[0, :, 0:d]
    scale = mod_ref[0, :, d:2 * d]
    gate_c = mod_ref[0, :, 2 * d:3 * d]
    x = x_ref[0]
    h = (x * (1.0 + scale) + shift).astype(BF16)
    hh = (xh_ref[0] * (1.0 + scale) + shift).astype(BF16)

    u_halo = jnp.dot(hh, wp_ref[:, :d], preferred_element_type=F32)
    u_s[0:POOL_HALO, :] = jnp.where(i > 0, u_halo, jnp.zeros_like(u_halo))
    u_s[POOL_HALO:POOL_HALO + tm, :] = jnp.dot(h, wp_ref[:, :d], preferred_element_type=F32)
    gates = jnp.dot(h, wp_ref[:, d:], preferred_element_type=F32)

    pos = i * tm + lax.broadcasted_iota(jnp.int32, (tm, 1), 0)
    branch_b = None
    for gi, w in enumerate(POOL_WINDOWS):
        cols = slice(gi * POOL_GROUP_WIDTH, (gi + 1) * POOL_GROUP_WIDTH)
        u = u_s[POOL_HALO:POOL_HALO + tm, cols]
        total = u
        for j in range(1, w):
            total = total + u_s[POOL_HALO - j:POOL_HALO - j + tm, cols]
        count = jnp.minimum(pos + 1, w).astype(F32)
        pm = total / count - u
        pg = jnp.dot(pm.astype(BF16), wpg_ref[gi], preferred_element_type=F32) * ps_ref[:, cols]
        part = jnp.dot(pg.astype(BF16), wbp_ref[cols, :], preferred_element_type=F32)
        branch_b = part if branch_b is None else branch_b + part

    branch_a = jnp.dot(att_ref[0], wba_ref[...], preferred_element_type=F32)
    merged = _sigmoid(gates[:, :d]) * branch_a + _sigmoid(gates[:, d:]) * branch_b
    mixer_out = jnp.dot(merged.astype(BF16), wout_ref[...], preferred_element_type=F32)
    o_ref[0] = _layer_norm(DEEPNORM_ALPHA * x + gate_c * mixer_out, lg_ref[...], lb_ref[...])


def _const_spec(shape):
    return pl.BlockSpec(shape, lambda b, i: (0,) * len(shape))


def _mixer(x, mod3, att, wp, wpg, pool_scale, wbp, wba, wout, ln_g, ln_b, *, tm):
    batch, seq, d = x.shape
    halo_blocks_per_tile = tm // POOL_HALO
    return pl.pallas_call(
        functools.partial(_mixer_kernel, tm=tm),
        grid=(batch, seq // tm),
        in_specs=[
            pl.BlockSpec((1, tm, d), lambda b, i: (b, i, 0)),
            pl.BlockSpec((1, POOL_HALO, d), lambda b, i: (b, jnp.maximum(i * halo_blocks_per_tile - 1, 0), 0)),
            pl.BlockSpec((1, 1, N_ADA * d), lambda b, i: (b, 0, 0)),
            pl.BlockSpec((1, tm, GROUP_WIDTH), lambda b, i: (b, i, 0)),
            _const_spec(wp.shape), _const_spec(wpg.shape), _const_spec(pool_scale.shape),
            _const_spec(wbp.shape), _const_spec(wba.shape), _const_spec(wout.shape),
            _const_spec(ln_g.shape), _const_spec(ln_b.shape),
        ],
        out_specs=pl.BlockSpec((1, tm, d), lambda b, i: (b, i, 0)),
        out_shape=jax.ShapeDtypeStruct((batch, seq, d), F32),
        scratch_shapes=[pltpu.VMEM((POOL_HALO + tm, d), F32)],
        compiler_params=pltpu.CompilerParams(
            dimension_semantics=("arbitrary", "arbitrary"),
            vmem_limit_bytes=SCOPED_VMEM_LIMIT_BYTES_V7X,
        ),
        name="mixer",
    )(x, x, mod3, att, wp, wpg, pool_scale, wbp, wba, wout, ln_g, ln_b)


def _ffn_kernel(x_ref, mod_ref, wg_ref, wu_ref, wd_ref, lg_ref, lb_ref, o_ref):
    d = D_MODEL
    shift = mod_ref[0, :, 3 * d:4 * d]
    scale = mod_ref[0, :, 4 * d:5 * d]
    gate_c = mod_ref[0, :, 5 * d:6 * d]
    x = x_ref[0]
    h = (x * (1.0 + scale) + shift).astype(BF16)
    gt = jnp.dot(h, wg_ref[...], preferred_element_type=F32)
    up = jnp.dot(h, wu_ref[...], preferred_element_type=F32)
    act = (gt * _sigmoid(gt) * up).astype(BF16)
    ffn = jnp.dot(act, wd_ref[...], preferred_element_type=F32)
    o_ref[0] = _layer_norm(DEEPNORM_ALPHA * x + gate_c * ffn, lg_ref[...], lb_ref[...])


def _ffn(x1, mod3, wg, wu, wd, ln_g, ln_b, *, tm):
    batch, seq, d = x1.shape
    return pl.pallas_call(
        _ffn_kernel,
        grid=(batch, seq // tm),
        in_specs=[
            pl.BlockSpec((1, tm, d), lambda b, i: (b, i, 0)),
            pl.BlockSpec((1, 1, N_ADA * d), lambda b, i: (b, 0, 0)),
            _const_spec(wg.shape), _const_spec(wu.shape), _const_spec(wd.shape),
            _const_spec(ln_g.shape), _const_spec(ln_b.shape),
        ],
        out_specs=pl.BlockSpec((1, tm, d), lambda b, i: (b, i, 0)),
        out_shape=jax.ShapeDtypeStruct((batch, seq, d), F32),
        compiler_params=pltpu.CompilerParams(
            dimension_semantics=("arbitrary", "arbitrary"),
            vmem_limit_bytes=SCOPED_VMEM_LIMIT_BYTES_V7X,
        ),
        name="ffn",
    )(x1, mod3, wg, wu, wd, ln_g, ln_b)


def kernel(x, c, w_ada, b_ada, w_in, w_branch_att, w_pool_group, pool_scale, w_branch_pool, w_out, ln1_g, ln1_b,
           w_gate, w_up, w_down, ln2_g, ln2_b):
    batch, seq, d = x.shape
    assert d == D_MODEL and w_ada.shape[0] == 1, "one layer of width D_MODEL"
    assert seq == BLOCK * DILATION_GROUPS[-1][1], "sequence must be one block per residue class of the widest dilation"
    l = 0
    mod = _ada(c, w_ada[l], b_ada[l])
    mod3 = mod.reshape(batch, 1, N_ADA * d)

    w_in_l = w_in[l]
    w_qkv = jnp.stack([
        jnp.concatenate([w_in_l[:, part * ATT_WIDTH + g * GROUP_WIDTH:part * ATT_WIDTH + (g + 1) * GROUP_WIDTH]
                         for part in range(3)], axis=1)
        for g in range(N_ATT_GROUPS)]).astype(BF16)
    qkv = _inproj(x, mod3, w_qkv)
    banded, paired = _bias_tables()
    att = _attend(qkv, jnp.asarray(banded), jnp.asarray(paired))

    x1 = _mixer(
        x, mod3, att,
        w_in_l[:, 3 * ATT_WIDTH:].astype(BF16), w_pool_group[l].astype(BF16), pool_scale[l].reshape(1, d),
        w_branch_pool[l].astype(BF16), w_branch_att[l].astype(BF16), w_out[l].astype(BF16),
        ln1_g[l].reshape(1, d), ln1_b[l].reshape(1, d), tm=512)
    return _ffn(x1, mod3, w_gate[l].astype(BF16), w_up[l].astype(BF16), w_down[l].astype(BF16),
                ln2_g[l].reshape(1, d), ln2_b[l].reshape(1, d), tm=512)
```

```python
import functools
import math

import jax
import jax.numpy as jnp
import numpy as np
from jax import lax
from jax.experimental import pallas as pl
from jax.experimental.pallas import tpu as pltpu

D_MODEL = 1024
HEAD_DIM = 64
HEADS_PER_GROUP = 4
DILATION_GROUPS = ((128, 1), (512, 4), (2048, 16))
N_ATT_GROUPS = len(DILATION_GROUPS)
N_ATT_HEADS = N_ATT_GROUPS * HEADS_PER_GROUP
ATT_WIDTH = N_ATT_HEADS * HEAD_DIM
GROUP_WIDTH = HEADS_PER_GROUP * HEAD_DIM
BLOCK = 128
POOL_WINDOWS = (2, 4, 8, 16)
POOL_GROUP_WIDTH = D_MODEL // len(POOL_WINDOWS)
POOL_HALO = 16
N_ADA = 6
DEEPNORM_ALPHA = 2.0 ** 0.25
LN_EPS = 1e-5

LANES_V7X = 128
SCOPED_VMEM_LIMIT_BYTES_V7X = 60000 * 1024

BF16 = jnp.bfloat16
F32 = jnp.float32


def _alibi_slopes(n):
    def pow2_slopes(m):
        start = 2.0 ** (-8.0 / m)
        return [start ** (i + 1) for i in range(m)]
    if math.log2(n).is_integer():
        s = pow2_slopes(n)
    else:
        c = 2 ** math.floor(math.log2(n))
        s = pow2_slopes(c) + pow2_slopes(2 * c)[0::2][: n - c]
    return np.array(sorted(s, reverse=True), dtype=np.float32)


def _bias_tables():
    slopes = _alibi_slopes(N_ATT_HEADS).reshape(N_ATT_GROUPS, HEADS_PER_GROUP)
    qi = np.arange(BLOCK) + BLOCK
    kj = np.arange(2 * BLOCK)
    diff = qi[:, None] - kj[None, :]
    banded, paired = [], []
    for g, (window, dilation) in enumerate(DILATION_GROUPS):
        steps = window // dilation
        valid = (diff >= 0) & (diff <= steps)
        dist = (diff * dilation).astype(np.float32)
        for h in range(HEADS_PER_GROUP):
            bias = -slopes[g, h] * dist
            normal = np.where(valid, bias, -np.inf).astype(np.float32)
            first = np.where(valid & (kj[None, :] >= BLOCK), bias, -np.inf).astype(np.float32)
            if _blocks_per_class(dilation) > 1:
                banded += [normal, first]
            else:
                own = normal[:, BLOCK:]
                off = np.full_like(own, -np.inf)
                paired.append(np.block([[own, off], [off, own]]))
    return np.stack(banded), np.stack(paired)


def _blocks_per_class(dilation):
    return DILATION_GROUPS[-1][1] // dilation


def _layer_norm(y, g, b):
    mu = jnp.mean(y, axis=-1, keepdims=True)
    yc = y - mu
    var = jnp.mean(yc * yc, axis=-1, keepdims=True)
    return yc * lax.rsqrt(var + LN_EPS) * g + b


def _sigmoid(x):
    return 1.0 / (1.0 + jnp.exp(-x))


def _ada_kernel(c_ref, w_ref, b_ref, o_ref):
    c = c_ref[...]
    s = (c * _sigmoid(c)).astype(BF16)
    o_ref[...] = jnp.dot(s, w_ref[...].astype(BF16), preferred_element_type=F32) + b_ref[...]


def _ada(c, w_ada, b_ada):
    batch, d = c.shape
    n = w_ada.shape[1]
    tn = d
    return pl.pallas_call(
        _ada_kernel,
        grid=(n // tn,),
        in_specs=[
            pl.BlockSpec((batch, d), lambda j: (0, 0)),
            pl.BlockSpec((d, tn), lambda j: (0, j)),
            pl.BlockSpec((1, tn), lambda j: (0, j)),
        ],
        out_specs=pl.BlockSpec((batch, tn), lambda j: (0, j)),
        out_shape=jax.ShapeDtypeStruct((batch, n), F32),
        name="ada",
    )(c, w_ada, b_ada.reshape(1, n))


def _inproj_kernel(*refs, seq, row_chunk):
    n_slab = D_MODEL // LANES_V7X
    x_refs = refs[:n_slab]
    mod_ref, w_ref, o_ref, h_ref = refs[n_slab:]
    g = pl.program_id(1)
    n_blocks = seq // BLOCK
    blocks_per_chunk = row_chunk // BLOCK
    n_row_chunks = seq // row_chunk

    def modulate_rows(dilation, chunk):
        blocks_per_class = n_blocks // dilation
        for k in range(chunk * blocks_per_chunk, (chunk + 1) * blocks_per_chunk):
            r, n = divmod(k, blocks_per_class)
            src = r + n * (BLOCK * dilation)
            for j in range(n_slab):
                lanes = slice(j * LANES_V7X, (j + 1) * LANES_V7X)
                if dilation == 1:
                    xs = x_refs[j][0, src:src + BLOCK, :]
                else:
                    xs = x_refs[j][0, pl.ds(src, BLOCK, stride=dilation), :]
                shift = mod_ref[0, :, lanes]
                scale = mod_ref[0, :, D_MODEL + j * LANES_V7X:D_MODEL + (j + 1) * LANES_V7X]
                h_ref[k * BLOCK:(k + 1) * BLOCK, lanes] = (xs * (1.0 + scale) + shift).astype(BF16)

    def project_rows(chunk):
        rows = slice(chunk * row_chunk, (chunk + 1) * row_chunk)
        proj = jnp.dot(h_ref[rows, :], w_ref[0], preferred_element_type=F32)
        q = proj[:, :GROUP_WIDTH] * (1.0 / math.sqrt(HEAD_DIM))
        o_ref[0, 0, rows, :GROUP_WIDTH] = q.astype(BF16)
        o_ref[0, 0, rows, GROUP_WIDTH:] = proj[:, GROUP_WIDTH:].astype(BF16)

    def run(dilation):
        modulate_rows(dilation, 0)
        for chunk in range(n_row_chunks):
            if chunk + 1 < n_row_chunks:
                modulate_rows(dilation, chunk + 1)
            project_rows(chunk)

    for gi, (_, dilation) in enumerate(DILATION_GROUPS):
        pl.when(g == gi)(functools.partial(run, dilation))


def _inproj(x, mod3, w_qkv):
    batch, seq, d = x.shape
    n_slab = d // LANES_V7X
    x_specs = [
        pl.BlockSpec((1, seq, LANES_V7X), functools.partial(lambda b, g, j: (b, 0, j), j=j))
        for j in range(n_slab)
    ]
    return pl.pallas_call(
        functools.partial(_inproj_kernel, seq=seq, row_chunk=512),
        grid=(batch, N_ATT_GROUPS),
        in_specs=x_specs + [
            pl.BlockSpec((1, 1, N_ADA * d), lambda b, g: (b, 0, 0)),
            pl.BlockSpec((1, d, 3 * GROUP_WIDTH), lambda b, g: (g, 0, 0)),
        ],
        out_specs=pl.BlockSpec((1, 1, seq, 3 * GROUP_WIDTH), lambda b, g: (g, b, 0, 0)),
        out_shape=jax.ShapeDtypeStruct((N_ATT_GROUPS, batch, seq, 3 * GROUP_WIDTH), BF16),
        scratch_shapes=[pltpu.VMEM((seq, d), BF16)],
        compiler_params=pltpu.CompilerParams(
            dimension_semantics=("arbitrary", "arbitrary"),
            vmem_limit_bytes=SCOPED_VMEM_LIMIT_BYTES_V7X,
        ),
        name="inproj",
    )(*([x] * n_slab), mod3, w_qkv)


def _attend_kernel(qkv_ref, banded_ref, paired_ref, o_ref, out_s, lse_s, *, seq):
    g = pl.program_id(1)
    n_blocks = seq // BLOCK
    n_pairs = HEADS_PER_GROUP // 2
    ones = jnp.ones((2 * BLOCK, LANES_V7X), BF16)

    def softmax_pv(q2, k2, v2e, load_bias):
        rows = q2.shape[0]
        low_half = lax.broadcasted_iota(jnp.int32, (rows, LANES_V7X), 1) < HEAD_DIM
        accs, dens, maxes = [], [], []
        for hh in range(2):
            keep = low_half if hh == 0 else jnp.logical_not(low_half)
            qm = jnp.where(keep, q2, jnp.zeros_like(q2))
            s = lax.dot_general(qm, k2, (((1,), (1,)), ((), ())), preferred_element_type=F32)
            s = s + load_bias(hh)
            m = jnp.max(s, axis=-1, keepdims=True)
            p = jnp.exp(s - m).astype(BF16)
            o = jnp.dot(p, v2e, preferred_element_type=F32)
            accs.append(o[:, :LANES_V7X])
            dens.append(o[:, LANES_V7X:])
            maxes.append(jnp.broadcast_to(m, (rows, LANES_V7X)))
        acc = jnp.where(low_half, accs[0], accs[1])
        den = jnp.where(low_half, dens[0], dens[1])
        mx = jnp.where(low_half, maxes[0], maxes[1])
        return acc / den, mx + jnp.log(den)

    def cols(part, pair):
        c0 = part * GROUP_WIDTH + pair * LANES_V7X
        return slice(c0, c0 + LANES_V7X)

    def banded_loop(gi, dilation, table0):
        blocks_per_class = n_blocks // dilation

        def body(j, carry):
            row0 = pl.multiple_of(j * BLOCK, BLOCK)
            n = j % blocks_per_class
            r = j // blocks_per_class
            first = jnp.where(n == 0, 1, 0)
            prow0 = pl.multiple_of(jnp.maximum(j - 1, 0) * BLOCK, BLOCK)
            nat0 = r + n * (BLOCK * dilation)
            for pair in range(n_pairs):
                q2 = qkv_ref[0, 0, pl.ds(row0, BLOCK), cols(0, pair)]
                k2 = jnp.concatenate([qkv_ref[0, 0, pl.ds(prow0, BLOCK), cols(1, pair)],
                                      qkv_ref[0, 0, pl.ds(row0, BLOCK), cols(1, pair)]], axis=0)
                v2 = jnp.concatenate([qkv_ref[0, 0, pl.ds(prow0, BLOCK), cols(2, pair)],
                                      qkv_ref[0, 0, pl.ds(row0, BLOCK), cols(2, pair)]], axis=0)
                v2e = jnp.concatenate([v2, ones], axis=1)
                out, lse = softmax_pv(
                    q2, k2, v2e, lambda hh: banded_ref[table0 + (pair * 2 + hh) * 2 + first])
                if dilation == 1:
                    out_s[gi, pair, pl.ds(row0, BLOCK), :] = out
                    lse_s[gi, pair, pl.ds(row0, BLOCK), :] = lse
                else:
                    out_s[gi, pair, pl.ds(nat0, BLOCK, stride=dilation), :] = out
                    lse_s[gi, pair, pl.ds(nat0, BLOCK, stride=dilation), :] = lse
            return carry

        lax.fori_loop(0, n_blocks, body, 0, unroll=4)

    def paired_loop(gi, dilation, table0):
        def body(j, carry):
            row0 = pl.multiple_of(j * (2 * BLOCK), 2 * BLOCK)
            for pair in range(n_pairs):
                q2 = qkv_ref[0, 0, pl.ds(row0, 2 * BLOCK), cols(0, pair)]
                k2 = qkv_ref[0, 0, pl.ds(row0, 2 * BLOCK), cols(1, pair)]
                v2e = jnp.concatenate([qkv_ref[0, 0, pl.ds(row0, 2 * BLOCK), cols(2, pair)], ones], axis=1)
                out, lse = softmax_pv(q2, k2, v2e, lambda hh: paired_ref[table0 + pair * 2 + hh])
                for half in range(2):
                    rows = slice(half * BLOCK, (half + 1) * BLOCK)
                    dst = pl.ds(2 * j + half, BLOCK, stride=dilation)
                    out_s[gi, pair, dst, :] = out[rows]
                    lse_s[gi, pair, dst, :] = lse[rows]
            return carry

        lax.fori_loop(0, n_blocks // 2, body, 0, unroll=2)

    n_banded = n_paired = 0
    for gi, (_, dilation) in enumerate(DILATION_GROUPS):
        if _blocks_per_class(dilation) > 1:
            pl.when(g == gi)(functools.partial(banded_loop, gi, dilation, n_banded))
            n_banded += 2 * HEADS_PER_GROUP
        else:
            pl.when(g == gi)(functools.partial(paired_loop, gi, dilation, n_paired))
            n_paired += HEADS_PER_GROUP

    @pl.when(g == N_ATT_GROUPS - 1)
    def _merge():
        rows_per_step = 256
        for pair in range(n_pairs):
            for c in range(seq // rows_per_step):
                rows = slice(c * rows_per_step, (c + 1) * rows_per_step)
                lses = [lse_s[gi, pair, rows, :] for gi in range(N_ATT_GROUPS)]
                top = functools.reduce(jnp.maximum, lses)
                es = [jnp.exp(l - top) for l in lses]
                den = functools.reduce(lambda a, b: a + b, es)
                num = functools.reduce(
                    lambda a, b: a + b, [es[gi] * out_s[gi, pair, rows, :] for gi in range(N_ATT_GROUPS)])
                o_ref[0, rows, pair * LANES_V7X:(pair + 1) * LANES_V7X] = (num / den).astype(BF16)


def _attend(qkv, banded, paired):
    _, batch, seq, width = qkv.shape
    n_pairs = HEADS_PER_GROUP // 2
    return pl.pallas_call(
        functools.partial(_attend_kernel, seq=seq),
        grid=(batch, N_ATT_GROUPS),
        in_specs=[
            pl.BlockSpec((1, 1, seq, width), lambda b, g: (g, b, 0, 0)),
            pl.BlockSpec(banded.shape, lambda b, g: (0, 0, 0)),
            pl.BlockSpec(paired.shape, lambda b, g: (0, 0, 0)),
        ],
        out_specs=pl.BlockSpec((1, seq, GROUP_WIDTH), lambda b, g: (b, 0, 0)),
        out_shape=jax.ShapeDtypeStruct((batch, seq, GROUP_WIDTH), BF16),
        scratch_shapes=[
            pltpu.VMEM((N_ATT_GROUPS, n_pairs, seq, LANES_V7X), F32),
            pltpu.VMEM((N_ATT_GROUPS, n_pairs, seq, LANES_V7X), F32),
        ],
        compiler_params=pltpu.CompilerParams(
            dimension_semantics=("arbitrary", "arbitrary"),
            vmem_limit_bytes=SCOPED_VMEM_LIMIT_BYTES_V7X,
        ),
        name="attend",
    )(qkv, banded, paired)


def _mixer_kernel(x_ref, xh_ref, mod_ref, att_ref, wp_ref, wpg_ref, ps_ref, wbp_ref, wba_ref, wout_ref,
                  lg_ref, lb_ref, o_ref, u_s, *, tm, n_sub):
    i = pl.program_id(1)
    d = D_MODEL
    sub = tm // n_sub
    shift = mod_ref[0, :, 0:d]
    scale = mod_ref[0, :, d:2 * d]
    gate_c = mod_ref[0, :, 2 * d:3 * d]

    hh = (xh_ref[0] * (1.0 + scale) + shift).astype(BF16)
    u_halo = jnp.dot(hh, wp_ref[:, :d], preferred_element_type=F32)
    u_s[0:POOL_HALO, :] = jnp.where(i > 0, u_halo, jnp.zeros_like(u_halo))

    for s in range(n_sub):
        rows = slice(s * sub, (s + 1) * sub)
        base = POOL_HALO + s * sub
        x = x_ref[0, rows, :]
        h = (x * (1.0 + scale) + shift).astype(BF16)
        u_s[base:base + sub, :] = jnp.dot(h, wp_ref[:, :d], preferred_element_type=F32)
        gates = jnp.dot(h, wp_ref[:, d:], preferred_element_type=F32)

        pos = i * tm + s * sub + lax.broadcasted_iota(jnp.int32, (sub, 1), 0)
        pgs = []
        for gi, w in enumerate(POOL_WINDOWS):
            cols = slice(gi * POOL_GROUP_WIDTH, (gi + 1) * POOL_GROUP_WIDTH)
            u_ext = u_s[base - POOL_HALO:base + sub, cols]
            total = u_ext
            reach = 1
            while reach < w:
                total = total + pltpu.roll(total, reach, axis=0)
                reach *= 2
            u = u_ext[POOL_HALO:]
            total = total[POOL_HALO:]
            inv_count = 1.0 / jnp.minimum(pos + 1, w).astype(F32)
            pm = total * inv_count - u
            pg = jnp.dot(pm.astype(BF16), wpg_ref[gi], preferred_element_type=F32) * ps_ref[:, cols]
            pgs.append(pg.astype(BF16))
        branch_b = jnp.dot(jnp.concatenate(pgs, axis=1), wbp_ref[...], preferred_element_type=F32)

        branch_a = jnp.dot(att_ref[0, rows, :], wba_ref[...], preferred_element_type=F32)
        merged = _sigmoid(gates[:, :d]) * branch_a + _sigmoid(gates[:, d:]) * branch_b
        mixer_out = jnp.dot(merged.astype(BF16), wout_ref[...], preferred_element_type=F32)
        o_ref[0, rows, :] = _layer_norm(DEEPNORM_ALPHA * x + gate_c * mixer_out, lg_ref[...], lb_ref[...])


def _const_spec(shape):
    return pl.BlockSpec(shape, lambda b, i: (0,) * len(shape))


def _mixer(x, mod3, att, wp, wpg, pool_scale, wbp, wba, wout, ln_g, ln_b, *, tm, n_sub):
    batch, seq, d = x.shape
    halo_blocks_per_tile = tm // POOL_HALO
    return pl.pallas_call(
        functools.partial(_mixer_kernel, tm=tm, n_sub=n_sub),
        grid=(batch, seq // tm),
        in_specs=[
            pl.BlockSpec((1, tm, d), lambda b, i: (b, i, 0)),
            pl.BlockSpec((1, POOL_HALO, d), lambda b, i: (b, jnp.maximum(i * halo_blocks_per_tile - 1, 0), 0)),
            pl.BlockSpec((1, 1, N_ADA * d), lambda b, i: (b, 0, 0)),
            pl.BlockSpec((1, tm, GROUP_WIDTH), lambda b, i: (b, i, 0)),
            _const_spec(wp.shape), _const_spec(wpg.shape), _const_spec(pool_scale.shape),
            _const_spec(wbp.shape), _const_spec(wba.shape), _const_spec(wout.shape),
            _const_spec(ln_g.shape), _const_spec(ln_b.shape),
        ],
        out_specs=pl.BlockSpec((1, tm, d), lambda b, i: (b, i, 0)),
        out_shape=jax.ShapeDtypeStruct((batch, seq, d), F32),
        scratch_shapes=[pltpu.VMEM((POOL_HALO + tm, d), F32)],
        compiler_params=pltpu.CompilerParams(
            dimension_semantics=("arbitrary", "arbitrary"),
            vmem_limit_bytes=SCOPED_VMEM_LIMIT_BYTES_V7X,
        ),
        name="mixer",
    )(x, x, mod3, att, wp, wpg, pool_scale, wbp, wba, wout, ln_g, ln_b)


def _ffn_kernel(x_ref, mod_ref, wg_ref, wu_ref, wd_ref, lg_ref, lb_ref, o_ref, *, n_sub):
    d = D_MODEL
    shift = mod_ref[0, :, 3 * d:4 * d]
    scale = mod_ref[0, :, 4 * d:5 * d]
    gate_c = mod_ref[0, :, 5 * d:6 * d]
    sub = x_ref.shape[1] // n_sub
    for s in range(n_sub):
        rows = slice(s * sub, (s + 1) * sub)
        x = x_ref[0, rows, :]
        h = (x * (1.0 + scale) + shift).astype(BF16)
        gt = jnp.dot(h, wg_ref[...], preferred_element_type=F32)
        up = jnp.dot(h, wu_ref[...], preferred_element_type=F32)
        act = (gt * _sigmoid(gt) * up).astype(BF16)
        ffn = jnp.dot(act, wd_ref[...], preferred_element_type=F32)
        o_ref[0, rows, :] = _layer_norm(DEEPNORM_ALPHA * x + gate_c * ffn, lg_ref[...], lb_ref[...])


def _ffn(x1, mod3, wg, wu, wd, ln_g, ln_b, *, tm, n_sub):
    batch, seq, d = x1.shape
    return pl.pallas_call(
        functools.partial(_ffn_kernel, n_sub=n_sub),
        grid=(batch, seq // tm),
        in_specs=[
            pl.BlockSpec((1, tm, d), lambda b, i: (b, i, 0)),
            pl.BlockSpec((1, 1, N_ADA * d), lambda b, i: (b, 0, 0)),
            _const_spec(wg.shape), _const_spec(wu.shape), _const_spec(wd.shape),
            _const_spec(ln_g.shape), _const_spec(ln_b.shape),
        ],
        out_specs=pl.BlockSpec((1, tm, d), lambda b, i: (b, i, 0)),
        out_shape=jax.ShapeDtypeStruct((batch, seq, d), F32),
        compiler_params=pltpu.CompilerParams(
            dimension_semantics=("arbitrary", "arbitrary"),
            vmem_limit_bytes=SCOPED_VMEM_LIMIT_BYTES_V7X,
        ),
        name="ffn",
    )(x1, mod3, wg, wu, wd, ln_g, ln_b)


def kernel(x, c, w_ada, b_ada, w_in, w_branch_att, w_pool_group, pool_scale, w_branch_pool, w_out, ln1_g, ln1_b,
           w_gate, w_up, w_down, ln2_g, ln2_b):
    batch, seq, d = x.shape
    assert d == D_MODEL and w_ada.shape[0] == 1, "one layer of width D_MODEL"
    assert seq == BLOCK * DILATION_GROUPS[-1][1], "sequence must be one block per residue class of the widest dilation"
    l = 0
    mod = _ada(c, w_ada[l], b_ada[l])
    mod3 = mod.reshape(batch, 1, N_ADA * d)

    w_in_l = w_in[l]
    w_qkv = jnp.stack([
        jnp.concatenate([w_in_l[:, part * ATT_WIDTH + g * GROUP_WIDTH:part * ATT_WIDTH + (g + 1) * GROUP_WIDTH]
                         for part in range(3)], axis=1)
        for g in range(N_ATT_GROUPS)]).astype(BF16)
    qkv = _inproj(x, mod3, w_qkv)
    banded, paired = _bias_tables()
    att = _attend(qkv, jnp.asarray(banded), jnp.asarray(paired))

    x1 = _mixer(
        x, mod3, att,
        w_in_l[:, 3 * ATT_WIDTH:].astype(BF16), w_pool_group[l].astype(BF16), pool_scale[l].reshape(1, d),
        w_branch_pool[l].astype(BF16), w_branch_att[l].astype(BF16), w_out[l].astype(BF16),
        ln1_g[l].reshape(1, d), ln1_b[l].reshape(1, d), tm=1024, n_sub=2)
    return _ffn(x1, mod3, w_gate[l].astype(BF16), w_up[l].astype(BF16), w_down[l].astype(BF16),
                ln2_g[l].reshape(1, d), ln2_b[l].reshape(1, d), tm=1024, n_sub=4)
```

```python
import functools
import math

import jax
import jax.numpy as jnp
import numpy as np
from jax import lax
from jax.experimental import pallas as pl
from jax.experimental.pallas import tpu as pltpu

D_MODEL = 1024
HEAD_DIM = 64
HEADS_PER_GROUP = 4
DILATION_GROUPS = ((128, 1), (512, 4), (2048, 16))
N_ATT_GROUPS = len(DILATION_GROUPS)
N_ATT_HEADS = N_ATT_GROUPS * HEADS_PER_GROUP
ATT_WIDTH = N_ATT_HEADS * HEAD_DIM
GROUP_WIDTH = HEADS_PER_GROUP * HEAD_DIM
BLOCK = 128
POOL_WINDOWS = (2, 4, 8, 16)
POOL_GROUP_WIDTH = D_MODEL // len(POOL_WINDOWS)
POOL_HALO = 16
N_ADA = 6
DEEPNORM_ALPHA = 2.0 ** 0.25
LN_EPS = 1e-5

LANES_V7X = 128
SCOPED_VMEM_LIMIT_BYTES_V7X = 60000 * 1024

BF16 = jnp.bfloat16
F32 = jnp.float32


def _alibi_slopes(n):
    def pow2_slopes(m):
        start = 2.0 ** (-8.0 / m)
        return [start ** (i + 1) for i in range(m)]
    if math.log2(n).is_integer():
        s = pow2_slopes(n)
    else:
        c = 2 ** math.floor(math.log2(n))
        s = pow2_slopes(c) + pow2_slopes(2 * c)[0::2][: n - c]
    return np.array(sorted(s, reverse=True), dtype=np.float32)


def _bias_tables():
    slopes = _alibi_slopes(N_ATT_HEADS).reshape(N_ATT_GROUPS, HEADS_PER_GROUP)
    qi = np.arange(BLOCK) + BLOCK
    kj = np.arange(2 * BLOCK)
    diff = qi[:, None] - kj[None, :]
    banded, paired = [], []
    for g, (window, dilation) in enumerate(DILATION_GROUPS):
        steps = window // dilation
        valid = (diff >= 0) & (diff <= steps)
        dist = (diff * dilation).astype(np.float32)
        for h in range(HEADS_PER_GROUP):
            bias = -slopes[g, h] * dist
            normal = np.where(valid, bias, -np.inf).astype(np.float32)
            first = np.where(valid & (kj[None, :] >= BLOCK), bias, -np.inf).astype(np.float32)
            if _blocks_per_class(dilation) > 1:
                banded += [normal, first]
            else:
                own = normal[:, BLOCK:]
                off = np.full_like(own, -np.inf)
                paired.append(np.block([[own, off], [off, own]]))
    return np.stack(banded), np.stack(paired)


def _blocks_per_class(dilation):
    return DILATION_GROUPS[-1][1] // dilation


def _layer_norm(y, g, b):
    mu = jnp.mean(y, axis=-1, keepdims=True)
    yc = y - mu
    var = jnp.mean(yc * yc, axis=-1, keepdims=True)
    return yc * lax.rsqrt(var + LN_EPS) * g + b


def _sigmoid(x):
    return 1.0 / (1.0 + jnp.exp(-x))


def _ada_kernel(c_ref, w_ref, b_ref, o_ref):
    c = c_ref[...]
    s = (c * _sigmoid(c)).astype(BF16)
    o_ref[...] = jnp.dot(s, w_ref[...].astype(BF16), preferred_element_type=F32) + b_ref[...]


def _ada(c, w_ada, b_ada):
    batch, d = c.shape
    n = w_ada.shape[1]
    tn = d
    return pl.pallas_call(
        _ada_kernel,
        grid=(n // tn,),
        in_specs=[
            pl.BlockSpec((batch, d), lambda j: (0, 0)),
            pl.BlockSpec((d, tn), lambda j: (0, j)),
            pl.BlockSpec((1, tn), lambda j: (0, j)),
        ],
        out_specs=pl.BlockSpec((batch, tn), lambda j: (0, j)),
        out_shape=jax.ShapeDtypeStruct((batch, n), F32),
        name="ada",
    )(c, w_ada, b_ada.reshape(1, n))


def _inproj_kernel(*refs, seq, row_chunk):
    n_slab = D_MODEL // LANES_V7X
    x_refs = refs[:n_slab]
    mod_ref, w_ref, o_ref, h_ref, hf_ref = refs[n_slab:]
    n_blocks = seq // BLOCK
    blocks_per_chunk = row_chunk // BLOCK
    n_slots = h_ref.shape[0]
    gi_kept = N_ATT_GROUPS - 2
    prev_dilation = DILATION_GROUPS[gi_kept][1]
    ratio = DILATION_GROUPS[-1][1] // prev_dilation

    def modulate_rows(gi, chunk, slot):
        dilation = DILATION_GROUPS[gi][1]
        blocks_per_class = n_blocks // dilation
        for kk in range(blocks_per_chunk):
            k = chunk * blocks_per_chunk + kk
            r, n = divmod(k, blocks_per_class)
            src = r + n * (BLOCK * dilation)
            for j in range(n_slab):
                lanes = slice(j * LANES_V7X, (j + 1) * LANES_V7X)
                if gi == N_ATT_GROUPS - 1:
                    c, r_prev = divmod(r, prev_dilation)
                    start = r_prev * (seq // prev_dilation) + n * (BLOCK * ratio) + c
                    hv = hf_ref[j, pl.ds(start, BLOCK, stride=ratio), :]
                else:
                    if dilation == 1:
                        xs = x_refs[j][0, src:src + BLOCK, :]
                    else:
                        xs = x_refs[j][0, pl.ds(src, BLOCK, stride=dilation), :]
                    shift = mod_ref[0, :, lanes]
                    scale = mod_ref[0, :, D_MODEL + j * LANES_V7X:D_MODEL + (j + 1) * LANES_V7X]
                    hv = xs * (1.0 + scale) + shift
                    if gi == gi_kept:
                        hf_ref[j, k * BLOCK:(k + 1) * BLOCK, :] = hv
                h_ref[slot, kk * BLOCK:(kk + 1) * BLOCK, lanes] = hv.astype(BF16)

    def project_rows(gi, chunk, slot):
        rows = slice(chunk * row_chunk, (chunk + 1) * row_chunk)
        proj = jnp.dot(h_ref[slot], w_ref[gi], preferred_element_type=F32)
        q = proj[:, :GROUP_WIDTH] * (1.0 / math.sqrt(HEAD_DIM))
        o_ref[gi, 0, rows, :GROUP_WIDTH] = q.astype(BF16)
        o_ref[gi, 0, rows, GROUP_WIDTH:] = proj[:, GROUP_WIDTH:].astype(BF16)

    items = [(gi, chunk) for gi in range(N_ATT_GROUPS) for chunk in range(seq // row_chunk)]
    modulate_rows(*items[0], 0)
    for n, item in enumerate(items):
        if n + 1 < len(items):
            modulate_rows(*items[n + 1], (n + 1) % n_slots)
        project_rows(*item, n % n_slots)


def _inproj(x, mod3, w_qkv):
    batch, seq, d = x.shape
    n_slab = d // LANES_V7X
    row_chunk = 512
    x_specs = [
        pl.BlockSpec((1, seq, LANES_V7X), functools.partial(lambda b, j: (b, 0, j), j=j))
        for j in range(n_slab)
    ]
    return pl.pallas_call(
        functools.partial(_inproj_kernel, seq=seq, row_chunk=row_chunk),
        grid=(batch,),
        in_specs=x_specs + [
            pl.BlockSpec((1, 1, N_ADA * d), lambda b: (b, 0, 0)),
            pl.BlockSpec(w_qkv.shape, lambda b: (0, 0, 0)),
        ],
        out_specs=pl.BlockSpec((N_ATT_GROUPS, 1, seq, 3 * GROUP_WIDTH), lambda b: (0, b, 0, 0)),
        out_shape=jax.ShapeDtypeStruct((N_ATT_GROUPS, batch, seq, 3 * GROUP_WIDTH), BF16),
        scratch_shapes=[pltpu.VMEM((2, row_chunk, d), BF16), pltpu.VMEM((n_slab, seq, LANES_V7X), F32)],
        compiler_params=pltpu.CompilerParams(
            dimension_semantics=("arbitrary",),
            vmem_limit_bytes=SCOPED_VMEM_LIMIT_BYTES_V7X,
        ),
        name="inproj",
    )(*([x] * n_slab), mod3, w_qkv)


def _attend_kernel(qkv_ref, banded_ref, paired_ref, o_ref, out_s, lse_s, *, seq):
    g = pl.program_id(1)
    n_blocks = seq // BLOCK
    n_pairs = HEADS_PER_GROUP // 2
    ones = jnp.ones((2 * BLOCK, LANES_V7X), BF16)

    def softmax_pv(q2, k2, v2e, load_bias):
        rows = q2.shape[0]
        low_half = lax.broadcasted_iota(jnp.int32, (rows, LANES_V7X), 1) < HEAD_DIM
        accs, dens, maxes = [], [], []
        for hh in range(2):
            keep = low_half if hh == 0 else jnp.logical_not(low_half)
            qm = jnp.where(keep, q2, jnp.zeros_like(q2))
            s = lax.dot_general(qm, k2, (((1,), (1,)), ((), ())), preferred_element_type=F32)
            s = s + load_bias(hh)
            m = jnp.max(s, axis=-1, keepdims=True)
            p = jnp.exp(s - m).astype(BF16)
            o = jnp.dot(p, v2e, preferred_element_type=F32)
            accs.append(o[:, :LANES_V7X])
            dens.append(o[:, LANES_V7X:])
            maxes.append(jnp.broadcast_to(m, (rows, LANES_V7X)))
        acc = jnp.where(low_half, accs[0], accs[1])
        den = jnp.where(low_half, dens[0], dens[1])
        mx = jnp.where(low_half, maxes[0], maxes[1])
        return acc / den, mx + jnp.log(den)

    def cols(part, pair):
        c0 = part * GROUP_WIDTH + pair * LANES_V7X
        return slice(c0, c0 + LANES_V7X)

    def banded_loop(gi, dilation, table0):
        blocks_per_class = n_blocks // dilation

        def body(j, carry):
            row0 = pl.multiple_of(j * BLOCK, BLOCK)
            n = j % blocks_per_class
            r = j // blocks_per_class
            first = jnp.where(n == 0, 1, 0)
            prow0 = pl.multiple_of(jnp.maximum(j - 1, 0) * BLOCK, BLOCK)
            nat0 = r + n * (BLOCK * dilation)
            for pair in range(n_pairs):
                q2 = qkv_ref[0, 0, pl.ds(row0, BLOCK), cols(0, pair)]
                k2 = jnp.concatenate([qkv_ref[0, 0, pl.ds(prow0, BLOCK), cols(1, pair)],
                                      qkv_ref[0, 0, pl.ds(row0, BLOCK), cols(1, pair)]], axis=0)
                v2 = jnp.concatenate([qkv_ref[0, 0, pl.ds(prow0, BLOCK), cols(2, pair)],
                                      qkv_ref[0, 0, pl.ds(row0, BLOCK), cols(2, pair)]], axis=0)
                v2e = jnp.concatenate([v2, ones], axis=1)
                out, lse = softmax_pv(
                    q2, k2, v2e, lambda hh: banded_ref[table0 + (pair * 2 + hh) * 2 + first])
                if dilation == 1:
                    out_s[gi, pair, pl.ds(row0, BLOCK), :] = out
                    lse_s[gi, pair, pl.ds(row0, BLOCK), :] = lse
                else:
                    out_s[gi, pair, pl.ds(nat0, BLOCK, stride=dilation), :] = out
                    lse_s[gi, pair, pl.ds(nat0, BLOCK, stride=dilation), :] = lse
            return carry

        lax.fori_loop(0, n_blocks, body, 0, unroll=4)

    def paired_loop(gi, dilation, table0):
        def body(j, carry):
            row0 = pl.multiple_of(j * (2 * BLOCK), 2 * BLOCK)
            for pair in range(n_pairs):
                q2 = qkv_ref[0, 0, pl.ds(row0, 2 * BLOCK), cols(0, pair)]
                k2 = qkv_ref[0, 0, pl.ds(row0, 2 * BLOCK), cols(1, pair)]
                v2e = jnp.concatenate([qkv_ref[0, 0, pl.ds(row0, 2 * BLOCK), cols(2, pair)], ones], axis=1)
                out, lse = softmax_pv(q2, k2, v2e, lambda hh: paired_ref[table0 + pair * 2 + hh])
                for half in range(2):
                    rows = slice(half * BLOCK, (half + 1) * BLOCK)
                    dst = pl.ds(2 * j + half, BLOCK, stride=dilation)
                    out_s[gi, pair, dst, :] = out[rows]
                    lse_s[gi, pair, dst, :] = lse[rows]
            return carry

        lax.fori_loop(0, n_blocks // 2, body, 0, unroll=2)

    n_banded = n_paired = 0
    for gi, (_, dilation) in enumerate(DILATION_GROUPS):
        if _blocks_per_class(dilation) > 1:
            pl.when(g == gi)(functools.partial(banded_loop, gi, dilation, n_banded))
            n_banded += 2 * HEADS_PER_GROUP
        else:
            pl.when(g == gi)(functools.partial(paired_loop, gi, dilation, n_paired))
            n_paired += HEADS_PER_GROUP

    @pl.when(g == N_ATT_GROUPS - 1)
    def _merge():
        rows_per_step = 256
        for pair in range(n_pairs):
            for c in range(seq // rows_per_step):
                rows = slice(c * rows_per_step, (c + 1) * rows_per_step)
                lses = [lse_s[gi, pair, rows, :] for gi in range(N_ATT_GROUPS)]
                top = functools.reduce(jnp.maximum, lses)
                es = [jnp.exp(l - top) for l in lses]
                den = functools.reduce(lambda a, b: a + b, es)
                num = functools.reduce(
                    lambda a, b: a + b, [es[gi] * out_s[gi, pair, rows, :] for gi in range(N_ATT_GROUPS)])
                o_ref[0, rows, pair * LANES_V7X:(pair + 1) * LANES_V7X] = (num / den).astype(BF16)


def _attend(qkv, banded, paired):
    _, batch, seq, width = qkv.shape
    n_pairs = HEADS_PER_GROUP // 2
    return pl.pallas_call(
        functools.partial(_attend_kernel, seq=seq),
        grid=(batch, N_ATT_GROUPS),
        in_specs=[
            pl.BlockSpec((1, 1, seq, width), lambda b, g: (g, b, 0, 0)),
            pl.BlockSpec(banded.shape, lambda b, g: (0, 0, 0)),
            pl.BlockSpec(paired.shape, lambda b, g: (0, 0, 0)),
        ],
        out_specs=pl.BlockSpec((1, seq, GROUP_WIDTH), lambda b, g: (b, 0, 0)),
        out_shape=jax.ShapeDtypeStruct((batch, seq, GROUP_WIDTH), BF16),
        scratch_shapes=[
            pltpu.VMEM((N_ATT_GROUPS, n_pairs, seq, LANES_V7X), F32),
            pltpu.VMEM((N_ATT_GROUPS, n_pairs, seq, LANES_V7X), F32),
        ],
        compiler_params=pltpu.CompilerParams(
            dimension_semantics=("arbitrary", "arbitrary"),
            vmem_limit_bytes=SCOPED_VMEM_LIMIT_BYTES_V7X,
        ),
        name="attend",
    )(qkv, banded, paired)


def _mixer_kernel(x_ref, xh_ref, mod_ref, att_ref, wp_ref, wpg_ref, ps_ref, wbp_ref, wba_ref, wout_ref,
                  lg_ref, lb_ref, o_ref, u_s, *, tm, n_sub):
    i = pl.program_id(1)
    d = D_MODEL
    sub = tm // n_sub
    shift = mod_ref[0, :, 0:d]
    scale = mod_ref[0, :, d:2 * d]
    gate_c = mod_ref[0, :, 2 * d:3 * d]

    hh = (xh_ref[0] * (1.0 + scale) + shift).astype(BF16)
    u_halo = jnp.dot(hh, wp_ref[:, :d], preferred_element_type=F32)
    u_s[0:POOL_HALO, :] = jnp.where(i > 0, u_halo, jnp.zeros_like(u_halo))

    for s in range(n_sub):
        rows = slice(s * sub, (s + 1) * sub)
        base = POOL_HALO + s * sub
        x = x_ref[0, rows, :]
        h = (x * (1.0 + scale) + shift).astype(BF16)
        u_s[base:base + sub, :] = jnp.dot(h, wp_ref[:, :d], preferred_element_type=F32)
        gates = jnp.dot(h, wp_ref[:, d:], preferred_element_type=F32)

        pos = i * tm + s * sub + lax.broadcasted_iota(jnp.int32, (sub, 1), 0)
        pgs = []
        for gi, w in enumerate(POOL_WINDOWS):
            cols = slice(gi * POOL_GROUP_WIDTH, (gi + 1) * POOL_GROUP_WIDTH)
            u_ext = u_s[base - POOL_HALO:base + sub, cols]
            total = u_ext
            reach = 1
            while reach < w:
                total = total + pltpu.roll(total, reach, axis=0)
                reach *= 2
            u = u_ext[POOL_HALO:]
            total = total[POOL_HALO:]
            inv_count = 1.0 / jnp.minimum(pos + 1, w).astype(F32)
            pm = total * inv_count - u
            pg = jnp.dot(pm.astype(BF16), wpg_ref[gi], preferred_element_type=F32) * ps_ref[:, cols]
            pgs.append(pg.astype(BF16))
        branch_b = jnp.dot(jnp.concatenate(pgs, axis=1), wbp_ref[...], preferred_element_type=F32)

        branch_a = jnp.dot(att_ref[0, rows, :], wba_ref[...], preferred_element_type=F32)
        merged = _sigmoid(gates[:, :d]) * branch_a + _sigmoid(gates[:, d:]) * branch_b
        mixer_out = jnp.dot(merged.astype(BF16), wout_ref[...], preferred_element_type=F32)
        o_ref[0, rows, :] = _layer_norm(DEEPNORM_ALPHA * x + gate_c * mixer_out, lg_ref[...], lb_ref[...])


def _const_spec(shape):
    return pl.BlockSpec(shape, lambda b, i: (0,) * len(shape))


def _mixer(x, mod3, att, wp, wpg, pool_scale, wbp, wba, wout, ln_g, ln_b, *, tm, n_sub):
    batch, seq, d = x.shape
    halo_blocks_per_tile = tm // POOL_HALO
    return pl.pallas_call(
        functools.partial(_mixer_kernel, tm=tm, n_sub=n_sub),
        grid=(batch, seq // tm),
        in_specs=[
            pl.BlockSpec((1, tm, d), lambda b, i: (b, i, 0)),
            pl.BlockSpec((1, POOL_HALO, d), lambda b, i: (b, jnp.maximum(i * halo_blocks_per_tile - 1, 0), 0)),
            pl.BlockSpec((1, 1, N_ADA * d), lambda b, i: (b, 0, 0)),
            pl.BlockSpec((1, tm, GROUP_WIDTH), lambda b, i: (b, i, 0)),
            _const_spec(wp.shape), _const_spec(wpg.shape), _const_spec(pool_scale.shape),
            _const_spec(wbp.shape), _const_spec(wba.shape), _const_spec(wout.shape),
            _const_spec(ln_g.shape), _const_spec(ln_b.shape),
        ],
        out_specs=pl.BlockSpec((1, tm, d), lambda b, i: (b, i, 0)),
        out_shape=jax.ShapeDtypeStruct((batch, seq, d), F32),
        scratch_shapes=[pltpu.VMEM((POOL_HALO + tm, d), F32)],
        compiler_params=pltpu.CompilerParams(
            dimension_semantics=("arbitrary", "arbitrary"),
            vmem_limit_bytes=SCOPED_VMEM_LIMIT_BYTES_V7X,
        ),
        name="mixer",
    )(x, x, mod3, att, wp, wpg, pool_scale, wbp, wba, wout, ln_g, ln_b)


def _ffn_kernel(x_ref, mod_ref, wg_ref, wu_ref, wd_ref, lg_ref, lb_ref, o_ref, *, n_sub):
    d = D_MODEL
    shift = mod_ref[0, :, 3 * d:4 * d]
    scale = mod_ref[0, :, 4 * d:5 * d]
    gate_c = mod_ref[0, :, 5 * d:6 * d]
    sub = x_ref.shape[1] // n_sub
    for s in range(n_sub):
        rows = slice(s * sub, (s + 1) * sub)
        x = x_ref[0, rows, :]
        h = (x * (1.0 + scale) + shift).astype(BF16)
        gt = jnp.dot(h, wg_ref[...], preferred_element_type=F32)
        up = jnp.dot(h, wu_ref[...], preferred_element_type=F32)
        act = (gt * _sigmoid(gt) * up).astype(BF16)
        ffn = jnp.dot(act, wd_ref[...], preferred_element_type=F32)
        o_ref[0, rows, :] = _layer_norm(DEEPNORM_ALPHA * x + gate_c * ffn, lg_ref[...], lb_ref[...])


def _ffn(x1, mod3, wg, wu, wd, ln_g, ln_b, *, tm, n_sub):
    batch, seq, d = x1.shape
    return pl.pallas_call(
        functools.partial(_ffn_kernel, n_sub=n_sub),
        grid=(batch, seq // tm),
        in_specs=[
            pl.BlockSpec((1, tm, d), lambda b, i: (b, i, 0)),
            pl.BlockSpec((1, 1, N_ADA * d), lambda b, i: (b, 0, 0)),
            _const_spec(wg.shape), _const_spec(wu.shape), _const_spec(wd.shape),
            _const_spec(ln_g.shape), _const_spec(ln_b.shape),
        ],
        out_specs=pl.BlockSpec((1, tm, d), lambda b, i: (b, i, 0)),
        out_shape=jax.ShapeDtypeStruct((batch, seq, d), F32),
        compiler_params=pltpu.CompilerParams(
            dimension_semantics=("arbitrary", "arbitrary"),
            vmem_limit_bytes=SCOPED_VMEM_LIMIT_BYTES_V7X,
        ),
        name="ffn",
    )(x1, mod3, wg, wu, wd, ln_g, ln_b)


def kernel(x, c, w_ada, b_ada, w_in, w_branch_att, w_pool_group, pool_scale, w_branch_pool, w_out, ln1_g, ln1_b,
           w_gate, w_up, w_down, ln2_g, ln2_b):
    batch, seq, d = x.shape
    assert d == D_MODEL and w_ada.shape[0] == 1, "one layer of width D_MODEL"
    assert seq == BLOCK * DILATION_GROUPS[-1][1], "sequence must be one block per residue class of the widest dilation"
    l = 0
    mod = _ada(c, w_ada[l], b_ada[l])
    mod3 = mod.reshape(batch, 1, N_ADA * d)

    w_in_l = w_in[l]
    w_qkv = jnp.stack([
        jnp.concatenate([w_in_l[:, part * ATT_WIDTH + g * GROUP_WIDTH:part * ATT_WIDTH + (g + 1) * GROUP_WIDTH]
                         for part in range(3)], axis=1)
        for g in range(N_ATT_GROUPS)]).astype(BF16)
    qkv = _inproj(x, mod3, w_qkv)
    banded, paired = _bias_tables()
    att = _attend(qkv, jnp.asarray(banded), jnp.asarray(paired))

    x1 = _mixer(
        x, mod3, att,
        w_in_l[:, 3 * ATT_WIDTH:].astype(BF16), w_pool_group[l].astype(BF16), pool_scale[l].reshape(1, d),
        w_branch_pool[l].astype(BF16), w_branch_att[l].astype(BF16), w_out[l].astype(BF16),
        ln1_g[l].reshape(1, d), ln1_b[l].reshape(1, d), tm=1024, n_sub=2)
    return _ffn(x1, mod3, w_gate[l].astype(BF16), w_up[l].astype(BF16), w_down[l].astype(BF16),
                ln2_g[l].reshape(1, d), ln2_b[l].reshape(1, d), tm=1024, n_sub=4)
```

```python
import functools
import math

import jax
import jax.numpy as jnp
import numpy as np
from jax import lax
from jax.experimental import pallas as pl
from jax.experimental.pallas import tpu as pltpu

D_MODEL = 1024
HEAD_DIM = 64
HEADS_PER_GROUP = 4
DILATION_GROUPS = ((128, 1), (512, 4), (2048, 16))
N_ATT_GROUPS = len(DILATION_GROUPS)
N_ATT_HEADS = N_ATT_GROUPS * HEADS_PER_GROUP
ATT_WIDTH = N_ATT_HEADS * HEAD_DIM
GROUP_WIDTH = HEADS_PER_GROUP * HEAD_DIM
BLOCK = 128
POOL_WINDOWS = (2, 4, 8, 16)
POOL_GROUP_WIDTH = D_MODEL // len(POOL_WINDOWS)
POOL_HALO = 16
N_ADA = 6
DEEPNORM_ALPHA = 2.0 ** 0.25
LN_EPS = 1e-5

LANES_V7X = 128
SCOPED_VMEM_LIMIT_BYTES_V7X = 60000 * 1024

BF16 = jnp.bfloat16
F32 = jnp.float32


def _alibi_slopes(n):
    def pow2_slopes(m):
        start = 2.0 ** (-8.0 / m)
        return [start ** (i + 1) for i in range(m)]
    if math.log2(n).is_integer():
        s = pow2_slopes(n)
    else:
        c = 2 ** math.floor(math.log2(n))
        s = pow2_slopes(c) + pow2_slopes(2 * c)[0::2][: n - c]
    return np.array(sorted(s, reverse=True), dtype=np.float32)


def _bias_tables():
    slopes = _alibi_slopes(N_ATT_HEADS).reshape(N_ATT_GROUPS, HEADS_PER_GROUP)
    qi = np.arange(BLOCK) + BLOCK
    kj = np.arange(2 * BLOCK)
    diff = qi[:, None] - kj[None, :]
    banded, paired = [], []
    for g, (window, dilation) in enumerate(DILATION_GROUPS):
        steps = window // dilation
        valid = (diff >= 0) & (diff <= steps)
        dist = (diff * dilation).astype(np.float32)
        for h in range(HEADS_PER_GROUP):
            bias = -slopes[g, h] * dist
            normal = np.where(valid, bias, -np.inf).astype(np.float32)
            first = np.where(valid & (kj[None, :] >= BLOCK), bias, -np.inf).astype(np.float32)
            if _blocks_per_class(dilation) > 1:
                banded += [normal, first]
            else:
                own = normal[:, BLOCK:]
                off = np.full_like(own, -np.inf)
                paired.append(np.block([[own, off], [off, own]]))
    return np.stack(banded), np.stack(paired)


def _blocks_per_class(dilation):
    return DILATION_GROUPS[-1][1] // dilation


def _layer_norm(y, g, b):
    mu = jnp.mean(y, axis=-1, keepdims=True)
    yc = y - mu
    var = jnp.mean(yc * yc, axis=-1, keepdims=True)
    return yc * lax.rsqrt(var + LN_EPS) * g + b


def _sigmoid(x):
    return 1.0 / (1.0 + jnp.exp(-x))


def _ada_kernel(c_ref, w_ref, b_ref, o_ref):
    c = c_ref[...]
    s = (c * _sigmoid(c)).astype(BF16)
    o_ref[...] = jnp.dot(s, w_ref[...].astype(BF16), preferred_element_type=F32) + b_ref[...]


def _ada(c, w_ada, b_ada):
    batch, d = c.shape
    n = w_ada.shape[1]
    tn = d
    return pl.pallas_call(
        _ada_kernel,
        grid=(n // tn,),
        in_specs=[
            pl.BlockSpec((batch, d), lambda j: (0, 0)),
            pl.BlockSpec((d, tn), lambda j: (0, j)),
            pl.BlockSpec((1, tn), lambda j: (0, j)),
        ],
        out_specs=pl.BlockSpec((batch, tn), lambda j: (0, j)),
        out_shape=jax.ShapeDtypeStruct((batch, n), F32),
        name="ada",
    )(c, w_ada, b_ada.reshape(1, n))


def _inproj_kernel(*refs, seq, row_chunk):
    n_slab = D_MODEL // LANES_V7X
    x_refs = refs[:n_slab]
    mod_ref, w_ref, o_ref, h_ref, hf_ref = refs[n_slab:]
    n_blocks = seq // BLOCK
    blocks_per_chunk = row_chunk // BLOCK
    n_slots = h_ref.shape[0]
    gi_kept = N_ATT_GROUPS - 2
    prev_dilation = DILATION_GROUPS[gi_kept][1]
    ratio = DILATION_GROUPS[-1][1] // prev_dilation

    def modulate_rows(gi, chunk, slot):
        dilation = DILATION_GROUPS[gi][1]
        blocks_per_class = n_blocks // dilation
        for kk in range(blocks_per_chunk):
            k = chunk * blocks_per_chunk + kk
            r, n = divmod(k, blocks_per_class)
            src = r + n * (BLOCK * dilation)
            for j in range(n_slab):
                lanes = slice(j * LANES_V7X, (j + 1) * LANES_V7X)
                if gi == N_ATT_GROUPS - 1:
                    c, r_prev = divmod(r, prev_dilation)
                    start = r_prev * (seq // prev_dilation) + n * (BLOCK * ratio) + c
                    hv = hf_ref[j, pl.ds(start, BLOCK, stride=ratio), :]
                else:
                    if dilation == 1:
                        xs = x_refs[j][0, src:src + BLOCK, :]
                    else:
                        xs = x_refs[j][0, pl.ds(src, BLOCK, stride=dilation), :]
                    shift = mod_ref[0, :, lanes]
                    scale = mod_ref[0, :, D_MODEL + j * LANES_V7X:D_MODEL + (j + 1) * LANES_V7X]
                    hv = xs * (1.0 + scale) + shift
                    if gi == gi_kept:
                        hf_ref[j, k * BLOCK:(k + 1) * BLOCK, :] = hv
                h_ref[slot, kk * BLOCK:(kk + 1) * BLOCK, lanes] = hv.astype(BF16)

    def project_rows(gi, chunk, slot):
        rows = slice(chunk * row_chunk, (chunk + 1) * row_chunk)
        proj = jnp.dot(h_ref[slot], w_ref[gi], preferred_element_type=F32)
        q = proj[:, :GROUP_WIDTH] * (1.0 / math.sqrt(HEAD_DIM))
        o_ref[gi, 0, rows, :GROUP_WIDTH] = q.astype(BF16)
        o_ref[gi, 0, rows, GROUP_WIDTH:] = proj[:, GROUP_WIDTH:].astype(BF16)

    items = [(gi, chunk) for gi in range(N_ATT_GROUPS) for chunk in range(seq // row_chunk)]
    modulate_rows(*items[0], 0)
    for n, item in enumerate(items):
        if n + 1 < len(items):
            modulate_rows(*items[n + 1], (n + 1) % n_slots)
        project_rows(*item, n % n_slots)


def _inproj(x, mod3, w_qkv):
    batch, seq, d = x.shape
    n_slab = d // LANES_V7X
    row_chunk = 512
    x_specs = [
        pl.BlockSpec((1, seq, LANES_V7X), functools.partial(lambda b, j: (b, 0, j), j=j))
        for j in range(n_slab)
    ]
    return pl.pallas_call(
        functools.partial(_inproj_kernel, seq=seq, row_chunk=row_chunk),
        grid=(batch,),
        in_specs=x_specs + [
            pl.BlockSpec((1, 1, N_ADA * d), lambda b: (b, 0, 0)),
            pl.BlockSpec(w_qkv.shape, lambda b: (0, 0, 0)),
        ],
        out_specs=pl.BlockSpec((N_ATT_GROUPS, 1, seq, 3 * GROUP_WIDTH), lambda b: (0, b, 0, 0)),
        out_shape=jax.ShapeDtypeStruct((N_ATT_GROUPS, batch, seq, 3 * GROUP_WIDTH), BF16),
        scratch_shapes=[pltpu.VMEM((2, row_chunk, d), BF16), pltpu.VMEM((n_slab, seq, LANES_V7X), F32)],
        compiler_params=pltpu.CompilerParams(
            dimension_semantics=("arbitrary",),
            vmem_limit_bytes=SCOPED_VMEM_LIMIT_BYTES_V7X,
        ),
        name="inproj",
    )(*([x] * n_slab), mod3, w_qkv)


def _attend_kernel(qkv_ref, banded_ref, paired_ref, o_ref, out_s, lse_s, *, seq):
    g = pl.program_id(1)
    n_blocks = seq // BLOCK
    n_pairs = HEADS_PER_GROUP // 2
    ones = jnp.ones((2 * BLOCK, LANES_V7X), BF16)

    def softmax_pv(q2, k2, v2e, load_bias):
        rows = q2.shape[0]
        low_half = lax.broadcasted_iota(jnp.int32, (rows, LANES_V7X), 1) < HEAD_DIM
        accs, dens, maxes = [], [], []
        for hh in range(2):
            keep = low_half if hh == 0 else jnp.logical_not(low_half)
            qm = jnp.where(keep, q2, jnp.zeros_like(q2))
            s = lax.dot_general(qm, k2, (((1,), (1,)), ((), ())), preferred_element_type=F32)
            s = s + load_bias(hh)
            m = jnp.max(s, axis=-1, keepdims=True)
            p = jnp.exp(s - m).astype(BF16)
            o = jnp.dot(p, v2e, preferred_element_type=F32)
            accs.append(o[:, :LANES_V7X])
            dens.append(o[:, LANES_V7X:])
            maxes.append(jnp.broadcast_to(m, (rows, LANES_V7X)))
        acc = jnp.where(low_half, accs[0], accs[1])
        den = jnp.where(low_half, dens[0], dens[1])
        mx = jnp.where(low_half, maxes[0], maxes[1])
        return acc / den, mx + jnp.log(den)

    def cols(part, pair):
        c0 = part * GROUP_WIDTH + pair * LANES_V7X
        return slice(c0, c0 + LANES_V7X)

    def banded_loop(gi, dilation, table0):
        blocks_per_class = n_blocks // dilation

        def body(j, carry):
            row0 = pl.multiple_of(j * BLOCK, BLOCK)
            n = j % blocks_per_class
            r = j // blocks_per_class
            first = jnp.where(n == 0, 1, 0)
            prow0 = pl.multiple_of(jnp.maximum(j - 1, 0) * BLOCK, BLOCK)
            nat0 = r + n * (BLOCK * dilation)
            for pair in range(n_pairs):
                q2 = qkv_ref[0, 0, pl.ds(row0, BLOCK), cols(0, pair)]
                k2 = jnp.concatenate([qkv_ref[0, 0, pl.ds(prow0, BLOCK), cols(1, pair)],
                                      qkv_ref[0, 0, pl.ds(row0, BLOCK), cols(1, pair)]], axis=0)
                v2 = jnp.concatenate([qkv_ref[0, 0, pl.ds(prow0, BLOCK), cols(2, pair)],
                                      qkv_ref[0, 0, pl.ds(row0, BLOCK), cols(2, pair)]], axis=0)
                v2e = jnp.concatenate([v2, ones], axis=1)
                out, lse = softmax_pv(
                    q2, k2, v2e, lambda hh: banded_ref[table0 + (pair * 2 + hh) * 2 + first])
                if dilation == 1:
                    out_s[gi, pair, pl.ds(row0, BLOCK), :] = out
                    lse_s[gi, pair, pl.ds(row0, BLOCK), :] = lse
                else:
                    out_s[gi, pair, pl.ds(nat0, BLOCK, stride=dilation), :] = out
                    lse_s[gi, pair, pl.ds(nat0, BLOCK, stride=dilation), :] = lse
            return carry

        lax.fori_loop(0, n_blocks, body, 0, unroll=16)

    def paired_loop(gi, dilation, table0):
        def body(j, carry):
            row0 = pl.multiple_of(j * (2 * BLOCK), 2 * BLOCK)
            for pair in range(n_pairs):
                q2 = qkv_ref[0, 0, pl.ds(row0, 2 * BLOCK), cols(0, pair)]
                k2 = qkv_ref[0, 0, pl.ds(row0, 2 * BLOCK), cols(1, pair)]
                v2e = jnp.concatenate([qkv_ref[0, 0, pl.ds(row0, 2 * BLOCK), cols(2, pair)], ones], axis=1)
                out, lse = softmax_pv(q2, k2, v2e, lambda hh: paired_ref[table0 + pair * 2 + hh])
                for half in range(2):
                    rows = slice(half * BLOCK, (half + 1) * BLOCK)
                    dst = pl.ds(2 * j + half, BLOCK, stride=dilation)
                    out_s[gi, pair, dst, :] = out[rows]
                    lse_s[gi, pair, dst, :] = lse[rows]
            return carry

        lax.fori_loop(0, n_blocks // 2, body, 0, unroll=8)

    n_banded = n_paired = 0
    for gi, (_, dilation) in enumerate(DILATION_GROUPS):
        if _blocks_per_class(dilation) > 1:
            pl.when(g == gi)(functools.partial(banded_loop, gi, dilation, n_banded))
            n_banded += 2 * HEADS_PER_GROUP
        else:
            pl.when(g == gi)(functools.partial(paired_loop, gi, dilation, n_paired))
            n_paired += HEADS_PER_GROUP

    @pl.when(g == N_ATT_GROUPS - 1)
    def _merge():
        rows_per_step = 256
        for pair in range(n_pairs):
            for c in range(seq // rows_per_step):
                rows = slice(c * rows_per_step, (c + 1) * rows_per_step)
                lses = [lse_s[gi, pair, rows, :] for gi in range(N_ATT_GROUPS)]
                top = functools.reduce(jnp.maximum, lses)
                es = [jnp.exp(l - top) for l in lses]
                den = functools.reduce(lambda a, b: a + b, es)
                num = functools.reduce(
                    lambda a, b: a + b, [es[gi] * out_s[gi, pair, rows, :] for gi in range(N_ATT_GROUPS)])
                o_ref[0, rows, pair * LANES_V7X:(pair + 1) * LANES_V7X] = (num / den).astype(BF16)


def _attend(qkv, banded, paired):
    _, batch, seq, width = qkv.shape
    n_pairs = HEADS_PER_GROUP // 2
    return pl.pallas_call(
        functools.partial(_attend_kernel, seq=seq),
        grid=(batch, N_ATT_GROUPS),
        in_specs=[
            pl.BlockSpec((1, 1, seq, width), lambda b, g: (g, b, 0, 0)),
            pl.BlockSpec(banded.shape, lambda b, g: (0, 0, 0)),
            pl.BlockSpec(paired.shape, lambda b, g: (0, 0, 0)),
        ],
        out_specs=pl.BlockSpec((1, seq, GROUP_WIDTH), lambda b, g: (b, 0, 0)),
        out_shape=jax.ShapeDtypeStruct((batch, seq, GROUP_WIDTH), BF16),
        scratch_shapes=[
            pltpu.VMEM((N_ATT_GROUPS, n_pairs, seq, LANES_V7X), F32),
            pltpu.VMEM((N_ATT_GROUPS, n_pairs, seq, LANES_V7X), F32),
        ],
        compiler_params=pltpu.CompilerParams(
            dimension_semantics=("arbitrary", "arbitrary"),
            vmem_limit_bytes=SCOPED_VMEM_LIMIT_BYTES_V7X,
        ),
        name="attend",
    )(qkv, banded, paired)


def _mixer_kernel(x_ref, xh_ref, mod_ref, att_ref, wp_ref, wpg_ref, ps_ref, wbp_ref, wba_ref, wout_ref,
                  lg_ref, lb_ref, o_ref, u_s, *, tm, n_sub):
    i = pl.program_id(1)
    d = D_MODEL
    sub = tm // n_sub
    shift = mod_ref[0, :, 0:d]
    scale = mod_ref[0, :, d:2 * d]
    gate_c = mod_ref[0, :, 2 * d:3 * d]

    hh = (xh_ref[0] * (1.0 + scale) + shift).astype(BF16)
    u_halo = jnp.dot(hh, wp_ref[:, :d], preferred_element_type=F32)
    u_s[0:POOL_HALO, :] = jnp.where(i > 0, u_halo, jnp.zeros_like(u_halo))

    for s in range(n_sub):
        rows = slice(s * sub, (s + 1) * sub)
        base = POOL_HALO + s * sub
        x = x_ref[0, rows, :]
        h = (x * (1.0 + scale) + shift).astype(BF16)
        u_s[base:base + sub, :] = jnp.dot(h, wp_ref[:, :d], preferred_element_type=F32)
        gates = jnp.dot(h, wp_ref[:, d:], preferred_element_type=F32)

        pos = i * tm + s * sub + lax.broadcasted_iota(jnp.int32, (sub, 1), 0)
        pgs = []
        for gi, w in enumerate(POOL_WINDOWS):
            cols = slice(gi * POOL_GROUP_WIDTH, (gi + 1) * POOL_GROUP_WIDTH)
            u_ext = u_s[base - POOL_HALO:base + sub, cols]
            total = u_ext
            reach = 1
            while reach < w:
                total = total + pltpu.roll(total, reach, axis=0)
                reach *= 2
            u = u_ext[POOL_HALO:]
            total = total[POOL_HALO:]
            inv_count = 1.0 / jnp.minimum(pos + 1, w).astype(F32)
            pm = total * inv_count - u
            pg = jnp.dot(pm.astype(BF16), wpg_ref[gi], preferred_element_type=F32) * ps_ref[:, cols]
            pgs.append(pg.astype(BF16))
        branch_b = jnp.dot(jnp.concatenate(pgs, axis=1), wbp_ref[...], preferred_element_type=F32)

        branch_a = jnp.dot(att_ref[0, rows, :], wba_ref[...], preferred_element_type=F32)
        merged = _sigmoid(gates[:, :d]) * branch_a + _sigmoid(gates[:, d:]) * branch_b
        mixer_out = jnp.dot(merged.astype(BF16), wout_ref[...], preferred_element_type=F32)
        o_ref[0, rows, :] = _layer_norm(DEEPNORM_ALPHA * x + gate_c * mixer_out, lg_ref[...], lb_ref[...])


def _const_spec(shape):
    return pl.BlockSpec(shape, lambda b, i: (0,) * len(shape))


def _mixer(x, mod3, att, wp, wpg, pool_scale, wbp, wba, wout, ln_g, ln_b, *, tm, n_sub):
    batch, seq, d = x.shape
    halo_blocks_per_tile = tm // POOL_HALO
    return pl.pallas_call(
        functools.partial(_mixer_kernel, tm=tm, n_sub=n_sub),
        grid=(batch, seq // tm),
        in_specs=[
            pl.BlockSpec((1, tm, d), lambda b, i: (b, i, 0)),
            pl.BlockSpec((1, POOL_HALO, d), lambda b, i: (b, jnp.maximum(i * halo_blocks_per_tile - 1, 0), 0)),
            pl.BlockSpec((1, 1, N_ADA * d), lambda b, i: (b, 0, 0)),
            pl.BlockSpec((1, tm, GROUP_WIDTH), lambda b, i: (b, i, 0)),
            _const_spec(wp.shape), _const_spec(wpg.shape), _const_spec(pool_scale.shape),
            _const_spec(wbp.shape), _const_spec(wba.shape), _const_spec(wout.shape),
            _const_spec(ln_g.shape), _const_spec(ln_b.shape),
        ],
        out_specs=pl.BlockSpec((1, tm, d), lambda b, i: (b, i, 0)),
        out_shape=jax.ShapeDtypeStruct((batch, seq, d), F32),
        scratch_shapes=[pltpu.VMEM((POOL_HALO + tm, d), F32)],
        compiler_params=pltpu.CompilerParams(
            dimension_semantics=("arbitrary", "arbitrary"),
            vmem_limit_bytes=SCOPED_VMEM_LIMIT_BYTES_V7X,
        ),
        name="mixer",
    )(x, x, mod3, att, wp, wpg, pool_scale, wbp, wba, wout, ln_g, ln_b)


def _ffn_kernel(x_ref, mod_ref, wg_ref, wu_ref, wd_ref, lg_ref, lb_ref, o_ref, *, n_sub):
    d = D_MODEL
    shift = mod_ref[0, :, 3 * d:4 * d]
    scale = mod_ref[0, :, 4 * d:5 * d]
    gate_c = mod_ref[0, :, 5 * d:6 * d]
    sub = x_ref.shape[1] // n_sub
    for s in range(n_sub):
        rows = slice(s * sub, (s + 1) * sub)
        x = x_ref[0, rows, :]
        h = (x * (1.0 + scale) + shift).astype(BF16)
        gt = jnp.dot(h, wg_ref[...], preferred_element_type=F32)
        up = jnp.dot(h, wu_ref[...], preferred_element_type=F32)
        act = (gt * _sigmoid(gt) * up).astype(BF16)
        ffn = jnp.dot(act, wd_ref[...], preferred_element_type=F32)
        o_ref[0, rows, :] = _layer_norm(DEEPNORM_ALPHA * x + gate_c * ffn, lg_ref[...], lb_ref[...])


def _ffn(x1, mod3, wg, wu, wd, ln_g, ln_b, *, tm, n_sub):
    batch, seq, d = x1.shape
    return pl.pallas_call(
        functools.partial(_ffn_kernel, n_sub=n_sub),
        grid=(batch, seq // tm),
        in_specs=[
            pl.BlockSpec((1, tm, d), lambda b, i: (b, i, 0)),
            pl.BlockSpec((1, 1, N_ADA * d), lambda b, i: (b, 0, 0)),
            _const_spec(wg.shape), _const_spec(wu.shape), _const_spec(wd.shape),
            _const_spec(ln_g.shape), _const_spec(ln_b.shape),
        ],
        out_specs=pl.BlockSpec((1, tm, d), lambda b, i: (b, i, 0)),
        out_shape=jax.ShapeDtypeStruct((batch, seq, d), F32),
        compiler_params=pltpu.CompilerParams(
            dimension_semantics=("arbitrary", "arbitrary"),
            vmem_limit_bytes=SCOPED_VMEM_LIMIT_BYTES_V7X,
        ),
        name="ffn",
    )(x1, mod3, wg, wu, wd, ln_g, ln_b)


def kernel(x, c, w_ada, b_ada, w_in, w_branch_att, w_pool_group, pool_scale, w_branch_pool, w_out, ln1_g, ln1_b,
           w_gate, w_up, w_down, ln2_g, ln2_b):
    batch, seq, d = x.shape
    assert d == D_MODEL and w_ada.shape[0] == 1, "one layer of width D_MODEL"
    assert seq == BLOCK * DILATION_GROUPS[-1][1], "sequence must be one block per residue class of the widest dilation"
    l = 0
    mod = _ada(c, w_ada[l], b_ada[l])
    mod3 = mod.reshape(batch, 1, N_ADA * d)

    w_in_l = w_in[l]
    w_qkv = jnp.stack([
        jnp.concatenate([w_in_l[:, part * ATT_WIDTH + g * GROUP_WIDTH:part * ATT_WIDTH + (g + 1) * GROUP_WIDTH]
                         for part in range(3)], axis=1)
        for g in range(N_ATT_GROUPS)]).astype(BF16)
    qkv = _inproj(x, mod3, w_qkv)
    banded, paired = _bias_tables()
    att = _attend(qkv, jnp.asarray(banded), jnp.asarray(paired))

    x1 = _mixer(
        x, mod3, att,
        w_in_l[:, 3 * ATT_WIDTH:].astype(BF16), w_pool_group[l].astype(BF16), pool_scale[l].reshape(1, d),
        w_branch_pool[l].astype(BF16), w_branch_att[l].astype(BF16), w_out[l].astype(BF16),
        ln1_g[l].reshape(1, d), ln1_b[l].reshape(1, d), tm=1024, n_sub=4)
    return _ffn(x1, mod3, w_gate[l].astype(BF16), w_up[l].astype(BF16), w_down[l].astype(BF16),
                ln2_g[l].reshape(1, d), ln2_b[l].reshape(1, d), tm=1024, n_sub=4)
```

```python
import functools
import math

import jax
import jax.numpy as jnp
import numpy as np
from jax import lax
from jax.experimental import pallas as pl
from jax.experimental.pallas import tpu as pltpu

D_MODEL = 1024
HEAD_DIM = 64
HEADS_PER_GROUP = 4
DILATION_GROUPS = ((128, 1), (512, 4), (2048, 16))
N_ATT_GROUPS = len(DILATION_GROUPS)
N_ATT_HEADS = N_ATT_GROUPS * HEADS_PER_GROUP
ATT_WIDTH = N_ATT_HEADS * HEAD_DIM
GROUP_WIDTH = HEADS_PER_GROUP * HEAD_DIM
BLOCK = 128
POOL_WINDOWS = (2, 4, 8, 16)
POOL_GROUP_WIDTH = D_MODEL // len(POOL_WINDOWS)
POOL_HALO = 16
N_ADA = 6
DEEPNORM_ALPHA = 2.0 ** 0.25
LN_EPS = 1e-5

LANES_V7X = 128
SCOPED_VMEM_LIMIT_BYTES_V7X = 60000 * 1024

BF16 = jnp.bfloat16
F32 = jnp.float32


def _alibi_slopes(n):
    def pow2_slopes(m):
        start = 2.0 ** (-8.0 / m)
        return [start ** (i + 1) for i in range(m)]
    if math.log2(n).is_integer():
        s = pow2_slopes(n)
    else:
        c = 2 ** math.floor(math.log2(n))
        s = pow2_slopes(c) + pow2_slopes(2 * c)[0::2][: n - c]
    return np.array(sorted(s, reverse=True), dtype=np.float32)


def _bias_tables():
    slopes = _alibi_slopes(N_ATT_HEADS).reshape(N_ATT_GROUPS, HEADS_PER_GROUP)
    qi = np.arange(BLOCK) + BLOCK
    kj = np.arange(2 * BLOCK)
    diff = qi[:, None] - kj[None, :]
    banded, paired = [], []
    for g, (window, dilation) in enumerate(DILATION_GROUPS):
        steps = window // dilation
        valid = (diff >= 0) & (diff <= steps)
        dist = (diff * dilation).astype(np.float32)
        for h in range(HEADS_PER_GROUP):
            bias = -slopes[g, h] * dist
            normal = np.where(valid, bias, -np.inf).astype(np.float32)
            first = np.where(valid & (kj[None, :] >= BLOCK), bias, -np.inf).astype(np.float32)
            if _blocks_per_class(dilation) > 1:
                banded += [normal, first]
            else:
                own = normal[:, BLOCK:]
                off = np.full_like(own, -np.inf)
                paired.append(np.block([[own, off], [off, own]]))
    return np.stack(banded), np.stack(paired)


def _blocks_per_class(dilation):
    return DILATION_GROUPS[-1][1] // dilation


def _layer_norm(y, g, b):
    mu = jnp.mean(y, axis=-1, keepdims=True)
    yc = y - mu
    var = jnp.mean(yc * yc, axis=-1, keepdims=True)
    return yc * lax.rsqrt(var + LN_EPS) * g + b


def _sigmoid(x):
    return 1.0 / (1.0 + jnp.exp(-x))


def _ada_kernel(c_ref, w_ref, b_ref, o_ref):
    c = c_ref[...]
    s = (c * _sigmoid(c)).astype(BF16)
    o_ref[...] = jnp.dot(s, w_ref[...].astype(BF16), preferred_element_type=F32) + b_ref[...]


def _ada(c, w_ada, b_ada):
    batch, d = c.shape
    n = w_ada.shape[1]
    tn = d
    return pl.pallas_call(
        _ada_kernel,
        grid=(n // tn,),
        in_specs=[
            pl.BlockSpec((batch, d), lambda j: (0, 0)),
            pl.BlockSpec((d, tn), lambda j: (0, j)),
            pl.BlockSpec((1, tn), lambda j: (0, j)),
        ],
        out_specs=pl.BlockSpec((batch, tn), lambda j: (0, j)),
        out_shape=jax.ShapeDtypeStruct((batch, n), F32),
        name="ada",
    )(c, w_ada, b_ada.reshape(1, n))


def _inproj_kernel(*refs, seq, row_chunk):
    n_slab = D_MODEL // LANES_V7X
    x_refs = refs[:n_slab]
    mod_ref, w_ref, o_ref, h_ref, hf_ref = refs[n_slab:]
    n_blocks = seq // BLOCK
    blocks_per_chunk = row_chunk // BLOCK
    n_slots = h_ref.shape[0]
    gi_kept = N_ATT_GROUPS - 2
    prev_dilation = DILATION_GROUPS[gi_kept][1]
    ratio = DILATION_GROUPS[-1][1] // prev_dilation

    def modulate_rows(gi, chunk, slot):
        dilation = DILATION_GROUPS[gi][1]
        blocks_per_class = n_blocks // dilation
        for kk in range(blocks_per_chunk):
            k = chunk * blocks_per_chunk + kk
            r, n = divmod(k, blocks_per_class)
            src = r + n * (BLOCK * dilation)
            for j in range(n_slab):
                lanes = slice(j * LANES_V7X, (j + 1) * LANES_V7X)
                if gi == N_ATT_GROUPS - 1:
                    c, r_prev = divmod(r, prev_dilation)
                    start = r_prev * (seq // prev_dilation) + n * (BLOCK * ratio) + c
                    hv = hf_ref[j, pl.ds(start, BLOCK, stride=ratio), :]
                else:
                    if dilation == 1:
                        xs = x_refs[j][0, src:src + BLOCK, :]
                    else:
                        xs = x_refs[j][0, pl.ds(src, BLOCK, stride=dilation), :]
                    shift = mod_ref[0, :, lanes]
                    scale = mod_ref[0, :, D_MODEL + j * LANES_V7X:D_MODEL + (j + 1) * LANES_V7X]
                    hv = xs * (1.0 + scale) + shift
                    if gi == gi_kept:
                        hf_ref[j, k * BLOCK:(k + 1) * BLOCK, :] = hv
                h_ref[slot, kk * BLOCK:(kk + 1) * BLOCK, lanes] = hv.astype(BF16)

    def project_rows(gi, chunk, slot):
        rows = slice(chunk * row_chunk, (chunk + 1) * row_chunk)
        proj = jnp.dot(h_ref[slot], w_ref[gi], preferred_element_type=F32)
        q = proj[:, :GROUP_WIDTH] * (1.0 / math.sqrt(HEAD_DIM))
        o_ref[gi, 0, rows, :GROUP_WIDTH] = q.astype(BF16)
        o_ref[gi, 0, rows, GROUP_WIDTH:] = proj[:, GROUP_WIDTH:].astype(BF16)

    items = [(gi, chunk) for gi in range(N_ATT_GROUPS) for chunk in range(seq // row_chunk)]
    modulate_rows(*items[0], 0)
    for n, item in enumerate(items):
        if n + 1 < len(items):
            modulate_rows(*items[n + 1], (n + 1) % n_slots)
        project_rows(*item, n % n_slots)


def _inproj(x, mod3, w_qkv):
    batch, seq, d = x.shape
    n_slab = d // LANES_V7X
    row_chunk = 512
    x_specs = [
        pl.BlockSpec((1, seq, LANES_V7X), functools.partial(lambda b, j: (b, 0, j), j=j))
        for j in range(n_slab)
    ]
    return pl.pallas_call(
        functools.partial(_inproj_kernel, seq=seq, row_chunk=row_chunk),
        grid=(batch,),
        in_specs=x_specs + [
            pl.BlockSpec((1, 1, N_ADA * d), lambda b: (b, 0, 0)),
            pl.BlockSpec(w_qkv.shape, lambda b: (0, 0, 0)),
        ],
        out_specs=pl.BlockSpec((N_ATT_GROUPS, 1, seq, 3 * GROUP_WIDTH), lambda b: (0, b, 0, 0)),
        out_shape=jax.ShapeDtypeStruct((N_ATT_GROUPS, batch, seq, 3 * GROUP_WIDTH), BF16),
        scratch_shapes=[pltpu.VMEM((2, row_chunk, d), BF16), pltpu.VMEM((n_slab, seq, LANES_V7X), F32)],
        compiler_params=pltpu.CompilerParams(
            dimension_semantics=("arbitrary",),
            vmem_limit_bytes=SCOPED_VMEM_LIMIT_BYTES_V7X,
        ),
        name="inproj",
    )(*([x] * n_slab), mod3, w_qkv)


def _attend_kernel(qkv_ref, banded_ref, paired_ref, o_ref, out_s, lse_s, *, seq):
    stat_refs = (out_s, lse_s)
    n_blocks = seq // BLOCK
    n_pairs = HEADS_PER_GROUP // 2
    ones = jnp.ones((2 * BLOCK, LANES_V7X), BF16)

    def softmax_pv(q2, k2, v2e, load_bias):
        rows = q2.shape[0]
        low_half = lax.broadcasted_iota(jnp.int32, (rows, LANES_V7X), 1) < HEAD_DIM
        accs, dens, maxes = [], [], []
        for hh in range(2):
            keep = low_half if hh == 0 else jnp.logical_not(low_half)
            qm = jnp.where(keep, q2, jnp.zeros_like(q2))
            s = lax.dot_general(qm, k2, (((1,), (1,)), ((), ())), preferred_element_type=F32)
            s = s + load_bias(hh)
            m = jnp.max(s, axis=-1, keepdims=True)
            p = jnp.exp(s - m).astype(BF16)
            o = jnp.dot(p, v2e, preferred_element_type=F32)
            accs.append(o[:, :LANES_V7X])
            dens.append(o[:, LANES_V7X:])
            maxes.append(jnp.broadcast_to(m, (rows, LANES_V7X)))
        acc = jnp.where(low_half, accs[0], accs[1])
        den = jnp.where(low_half, dens[0], dens[1])
        mx = jnp.where(low_half, maxes[0], maxes[1])
        return acc / den, mx + jnp.log(den)

    def put(gi, pair, idx, stats, rows=slice(None)):
        for ref, value in zip(stat_refs, stats):
            ref[gi, pair, idx, :] = value[rows]

    def cols(part, pair):
        c0 = part * GROUP_WIDTH + pair * LANES_V7X
        return slice(c0, c0 + LANES_V7X)

    def banded_loop(gi, dilation, table0):
        blocks_per_class = n_blocks // dilation

        def body(j, carry):
            row0 = pl.multiple_of(j * BLOCK, BLOCK)
            n = j % blocks_per_class
            r = j // blocks_per_class
            first = jnp.where(n == 0, 1, 0)
            prow0 = pl.multiple_of(jnp.maximum(j - 1, 0) * BLOCK, BLOCK)
            nat0 = r + n * (BLOCK * dilation)
            for pair in range(n_pairs):
                q2 = qkv_ref[gi, 0,pl.ds(row0, BLOCK), cols(0, pair)]
                k2 = jnp.concatenate([qkv_ref[gi, 0,pl.ds(prow0, BLOCK), cols(1, pair)],
                                      qkv_ref[gi, 0,pl.ds(row0, BLOCK), cols(1, pair)]], axis=0)
                v2 = jnp.concatenate([qkv_ref[gi, 0,pl.ds(prow0, BLOCK), cols(2, pair)],
                                      qkv_ref[gi, 0,pl.ds(row0, BLOCK), cols(2, pair)]], axis=0)
                v2e = jnp.concatenate([v2, ones], axis=1)
                stats = softmax_pv(
                    q2, k2, v2e, lambda hh: banded_ref[table0 + (pair * 2 + hh) * 2 + first])
                if dilation == 1:
                    put(gi, pair, pl.ds(row0, BLOCK), stats)
                else:
                    put(gi, pair, pl.ds(nat0, BLOCK, stride=dilation), stats)
            return carry

        lax.fori_loop(0, n_blocks, body, 0, unroll=16)

    def paired_loop(gi, dilation, table0):
        def body(j, carry):
            row0 = pl.multiple_of(j * (2 * BLOCK), 2 * BLOCK)
            for pair in range(n_pairs):
                q2 = qkv_ref[gi, 0,pl.ds(row0, 2 * BLOCK), cols(0, pair)]
                k2 = qkv_ref[gi, 0,pl.ds(row0, 2 * BLOCK), cols(1, pair)]
                v2e = jnp.concatenate([qkv_ref[gi, 0,pl.ds(row0, 2 * BLOCK), cols(2, pair)], ones], axis=1)
                stats = softmax_pv(q2, k2, v2e, lambda hh: paired_ref[table0 + pair * 2 + hh])
                for half in range(2):
                    put(gi, pair, pl.ds(2 * j + half, BLOCK, stride=dilation), stats,
                        rows=slice(half * BLOCK, (half + 1) * BLOCK))
            return carry

        lax.fori_loop(0, n_blocks // 2, body, 0, unroll=8)

    n_banded = n_paired = 0
    for gi, (_, dilation) in enumerate(DILATION_GROUPS):
        if _blocks_per_class(dilation) > 1:
            banded_loop(gi, dilation, n_banded)
            n_banded += 2 * HEADS_PER_GROUP
        else:
            paired_loop(gi, dilation, n_paired)
            n_paired += HEADS_PER_GROUP

    rows_per_step = 256
    for pair in range(n_pairs):
        for c in range(seq // rows_per_step):
            rows = slice(c * rows_per_step, (c + 1) * rows_per_step)
            lses = [lse_s[gi, pair, rows, :] for gi in range(N_ATT_GROUPS)]
            top = functools.reduce(jnp.maximum, lses)
            es = [jnp.exp(l - top) for l in lses]
            den = functools.reduce(lambda a, b: a + b, es)
            num = functools.reduce(
                lambda a, b: a + b, [es[gi] * out_s[gi, pair, rows, :] for gi in range(N_ATT_GROUPS)])
            o_ref[0, rows, pair * LANES_V7X:(pair + 1) * LANES_V7X] = (num / den).astype(BF16)


def _attend(qkv, banded, paired):
    _, batch, seq, width = qkv.shape
    n_pairs = HEADS_PER_GROUP // 2
    return pl.pallas_call(
        functools.partial(_attend_kernel, seq=seq),
        grid=(batch,),
        in_specs=[
            pl.BlockSpec((N_ATT_GROUPS, 1, seq, width), lambda b: (0, b, 0, 0)),
            pl.BlockSpec(banded.shape, lambda b: (0, 0, 0)),
            pl.BlockSpec(paired.shape, lambda b: (0, 0, 0)),
        ],
        out_specs=pl.BlockSpec((1, seq, GROUP_WIDTH), lambda b: (b, 0, 0)),
        out_shape=jax.ShapeDtypeStruct((batch, seq, GROUP_WIDTH), BF16),
        scratch_shapes=[pltpu.VMEM((N_ATT_GROUPS, n_pairs, seq, LANES_V7X), F32)] * 2,
        compiler_params=pltpu.CompilerParams(
            dimension_semantics=("arbitrary",),
            vmem_limit_bytes=SCOPED_VMEM_LIMIT_BYTES_V7X,
        ),
        name="attend",
    )(qkv, banded, paired)


def _mixer_kernel(x_ref, xh_ref, mod_ref, att_ref, wp_ref, wpg_ref, ps_ref, wbp_ref, wba_ref, wout_ref,
                  lg_ref, lb_ref, o_ref, u_s, *, tm, n_sub):
    i = pl.program_id(1)
    d = D_MODEL
    sub = tm // n_sub
    shift = mod_ref[0, :, 0:d]
    scale = mod_ref[0, :, d:2 * d]
    gate_c = mod_ref[0, :, 2 * d:3 * d]

    hh = (xh_ref[0] * (1.0 + scale) + shift).astype(BF16)
    u_halo = jnp.dot(hh, wp_ref[:, :d], preferred_element_type=F32)
    u_s[0:POOL_HALO, :] = jnp.where(i > 0, u_halo, jnp.zeros_like(u_halo))

    for s in range(n_sub):
        rows = slice(s * sub, (s + 1) * sub)
        base = POOL_HALO + s * sub
        x = x_ref[0, rows, :]
        h = (x * (1.0 + scale) + shift).astype(BF16)
        u_s[base:base + sub, :] = jnp.dot(h, wp_ref[:, :d], preferred_element_type=F32)
        gates = jnp.dot(h, wp_ref[:, d:], preferred_element_type=F32)

        pos = i * tm + s * sub + lax.broadcasted_iota(jnp.int32, (sub, 1), 0)
        pgs = []
        for gi, w in enumerate(POOL_WINDOWS):
            cols = slice(gi * POOL_GROUP_WIDTH, (gi + 1) * POOL_GROUP_WIDTH)
            u_ext = u_s[base - POOL_HALO:base + sub, cols]
            total = u_ext
            reach = 1
            while reach < w:
                total = total + pltpu.roll(total, reach, axis=0)
                reach *= 2
            u = u_ext[POOL_HALO:]
            total = total[POOL_HALO:]
            inv_count = 1.0 / jnp.minimum(pos + 1, w).astype(F32)
            pm = total * inv_count - u
            pg = jnp.dot(pm.astype(BF16), wpg_ref[gi], preferred_element_type=F32) * ps_ref[:, cols]
            pgs.append(pg.astype(BF16))
        branch_b = jnp.dot(jnp.concatenate(pgs, axis=1), wbp_ref[...], preferred_element_type=F32)

        branch_a = jnp.dot(att_ref[0, rows, :], wba_ref[...], preferred_element_type=F32)
        merged = _sigmoid(gates[:, :d]) * branch_a + _sigmoid(gates[:, d:]) * branch_b
        mixer_out = jnp.dot(merged.astype(BF16), wout_ref[...], preferred_element_type=F32)
        o_ref[0, rows, :] = _layer_norm(DEEPNORM_ALPHA * x + gate_c * mixer_out, lg_ref[...], lb_ref[...])


def _const_spec(shape):
    return pl.BlockSpec(shape, lambda b, i: (0,) * len(shape))


def _mixer(x, mod3, att, wp, wpg, pool_scale, wbp, wba, wout, ln_g, ln_b, *, tm, n_sub):
    batch, seq, d = x.shape
    halo_blocks_per_tile = tm // POOL_HALO
    return pl.pallas_call(
        functools.partial(_mixer_kernel, tm=tm, n_sub=n_sub),
        grid=(batch, seq // tm),
        in_specs=[
            pl.BlockSpec((1, tm, d), lambda b, i: (b, i, 0)),
            pl.BlockSpec((1, POOL_HALO, d), lambda b, i: (b, jnp.maximum(i * halo_blocks_per_tile - 1, 0), 0)),
            pl.BlockSpec((1, 1, N_ADA * d), lambda b, i: (b, 0, 0)),
            pl.BlockSpec((1, tm, GROUP_WIDTH), lambda b, i: (b, i, 0)),
            _const_spec(wp.shape), _const_spec(wpg.shape), _const_spec(pool_scale.shape),
            _const_spec(wbp.shape), _const_spec(wba.shape), _const_spec(wout.shape),
            _const_spec(ln_g.shape), _const_spec(ln_b.shape),
        ],
        out_specs=pl.BlockSpec((1, tm, d), lambda b, i: (b, i, 0)),
        out_shape=jax.ShapeDtypeStruct((batch, seq, d), F32),
        scratch_shapes=[pltpu.VMEM((POOL_HALO + tm, d), F32)],
        compiler_params=pltpu.CompilerParams(
            dimension_semantics=("arbitrary", "arbitrary"),
            vmem_limit_bytes=SCOPED_VMEM_LIMIT_BYTES_V7X,
        ),
        name="mixer",
    )(x, x, mod3, att, wp, wpg, pool_scale, wbp, wba, wout, ln_g, ln_b)


def _ffn_kernel(x_ref, mod_ref, wg_ref, wu_ref, wd_ref, lg_ref, lb_ref, o_ref, *, n_sub):
    d = D_MODEL
    shift = mod_ref[0, :, 3 * d:4 * d]
    scale = mod_ref[0, :, 4 * d:5 * d]
    gate_c = mod_ref[0, :, 5 * d:6 * d]
    sub = x_ref.shape[1] // n_sub
    for s in range(n_sub):
        rows = slice(s * sub, (s + 1) * sub)
        x = x_ref[0, rows, :]
        h = (x * (1.0 + scale) + shift).astype(BF16)
        gt = jnp.dot(h, wg_ref[...], preferred_element_type=F32)
        up = jnp.dot(h, wu_ref[...], preferred_element_type=F32)
        act = (gt * _sigmoid(gt) * up).astype(BF16)
        ffn = jnp.dot(act, wd_ref[...], preferred_element_type=F32)
        o_ref[0, rows, :] = _layer_norm(DEEPNORM_ALPHA * x + gate_c * ffn, lg_ref[...], lb_ref[...])


def _ffn(x1, mod3, wg, wu, wd, ln_g, ln_b, *, tm, n_sub):
    batch, seq, d = x1.shape
    return pl.pallas_call(
        functools.partial(_ffn_kernel, n_sub=n_sub),
        grid=(batch, seq // tm),
        in_specs=[
            pl.BlockSpec((1, tm, d), lambda b, i: (b, i, 0)),
            pl.BlockSpec((1, 1, N_ADA * d), lambda b, i: (b, 0, 0)),
            _const_spec(wg.shape), _const_spec(wu.shape), _const_spec(wd.shape),
            _const_spec(ln_g.shape), _const_spec(ln_b.shape),
        ],
        out_specs=pl.BlockSpec((1, tm, d), lambda b, i: (b, i, 0)),
        out_shape=jax.ShapeDtypeStruct((batch, seq, d), F32),
        compiler_params=pltpu.CompilerParams(
            dimension_semantics=("arbitrary", "arbitrary"),
            vmem_limit_bytes=SCOPED_VMEM_LIMIT_BYTES_V7X,
        ),
        name="ffn",
    )(x1, mod3, wg, wu, wd, ln_g, ln_b)


def kernel(x, c, w_ada, b_ada, w_in, w_branch_att, w_pool_group, pool_scale, w_branch_pool, w_out, ln1_g, ln1_b,
           w_gate, w_up, w_down, ln2_g, ln2_b):
    batch, seq, d = x.shape
    assert d == D_MODEL and w_ada.shape[0] == 1, "one layer of width D_MODEL"
    assert seq == BLOCK * DILATION_GROUPS[-1][1], "sequence must be one block per residue class of the widest dilation"
    l = 0
    mod = _ada(c, w_ada[l], b_ada[l])
    mod3 = mod.reshape(batch, 1, N_ADA * d)

    w_in_l = w_in[l]
    w_qkv = jnp.stack([
        jnp.concatenate([w_in_l[:, part * ATT_WIDTH + g * GROUP_WIDTH:part * ATT_WIDTH + (g + 1) * GROUP_WIDTH]
                         for part in range(3)], axis=1)
        for g in range(N_ATT_GROUPS)]).astype(BF16)
    qkv = _inproj(x, mod3, w_qkv)
    banded, paired = _bias_tables()
    att = _attend(qkv, jnp.asarray(banded), jnp.asarray(paired))

    x1 = _mixer(
        x, mod3, att,
        w_in_l[:, 3 * ATT_WIDTH:].astype(BF16), w_pool_group[l].astype(BF16), pool_scale[l].reshape(1, d),
        w_branch_pool[l].astype(BF16), w_branch_att[l].astype(BF16), w_out[l].astype(BF16),
        ln1_g[l].reshape(1, d), ln1_b[l].reshape(1, d), tm=1024, n_sub=4)
    return _ffn(x1, mod3, w_gate[l].astype(BF16), w_up[l].astype(BF16), w_down[l].astype(BF16),
                ln2_g[l].reshape(1, d), ln2_b[l].reshape(1, d), tm=1024, n_sub=4)
```

```python
import functools
import math

import jax
import jax.numpy as jnp
import numpy as np
from jax import lax
from jax.experimental import pallas as pl
from jax.experimental.pallas import tpu as pltpu

D_MODEL = 1024
HEAD_DIM = 64
HEADS_PER_GROUP = 4
DILATION_GROUPS = ((128, 1), (512, 4), (2048, 16))
N_ATT_GROUPS = len(DILATION_GROUPS)
N_ATT_HEADS = N_ATT_GROUPS * HEADS_PER_GROUP
ATT_WIDTH = N_ATT_HEADS * HEAD_DIM
GROUP_WIDTH = HEADS_PER_GROUP * HEAD_DIM
BLOCK = 128
POOL_WINDOWS = (2, 4, 8, 16)
POOL_GROUP_WIDTH = D_MODEL // len(POOL_WINDOWS)
POOL_HALO = 16
N_ADA = 6
DEEPNORM_ALPHA = 2.0 ** 0.25
LN_EPS = 1e-5

LANES_V7X = 128
SCOPED_VMEM_LIMIT_BYTES_V7X = 60000 * 1024

INPROJ_ROW_CHUNK = 512
MERGE_ROWS = 256
MIXER_TILE_ROWS, MIXER_SUBTILES = 1024, 4
FFN_TILE_ROWS, FFN_SUBTILES = 1024, 4

BF16 = jnp.bfloat16
F32 = jnp.float32


def _alibi_slopes(n):
    def pow2_slopes(m):
        start = 2.0 ** (-8.0 / m)
        return [start ** (i + 1) for i in range(m)]
    if math.log2(n).is_integer():
        s = pow2_slopes(n)
    else:
        c = 2 ** math.floor(math.log2(n))
        s = pow2_slopes(c) + pow2_slopes(2 * c)[0::2][: n - c]
    return np.array(sorted(s, reverse=True), dtype=np.float32)


def _bias_tables():
    slopes = _alibi_slopes(N_ATT_HEADS).reshape(N_ATT_GROUPS, HEADS_PER_GROUP)
    qi = np.arange(BLOCK) + BLOCK
    kj = np.arange(2 * BLOCK)
    diff = qi[:, None] - kj[None, :]
    banded, paired = [], []
    for g, (window, dilation) in enumerate(DILATION_GROUPS):
        steps = window // dilation
        valid = (diff >= 0) & (diff <= steps)
        dist = (diff * dilation).astype(np.float32)
        for h in range(HEADS_PER_GROUP):
            bias = -slopes[g, h] * dist
            normal = np.where(valid, bias, -np.inf).astype(np.float32)
            first = np.where(valid & (kj[None, :] >= BLOCK), bias, -np.inf).astype(np.float32)
            if _blocks_per_class(dilation) > 1:
                banded += [normal, first]
            else:
                own = normal[:, BLOCK:]
                off = np.full_like(own, -np.inf)
                paired.append(np.block([[own, off], [off, own]]))
    return np.stack(banded), np.stack(paired)


def _blocks_per_class(dilation):
    return DILATION_GROUPS[-1][1] // dilation


def _layer_norm(y, g, b):
    mu = jnp.mean(y, axis=-1, keepdims=True)
    yc = y - mu
    var = jnp.mean(yc * yc, axis=-1, keepdims=True)
    return yc * lax.rsqrt(var + LN_EPS) * g + b


def _sigmoid(x):
    return 1.0 / (1.0 + jnp.exp(-x))


def _ada_kernel(c_ref, w_ref, b_ref, o_ref):
    c = c_ref[...]
    s = (c * _sigmoid(c)).astype(BF16)
    o_ref[...] = jnp.dot(s, w_ref[...].astype(BF16), preferred_element_type=F32) + b_ref[...]


def _ada(c, w_ada, b_ada):
    batch, d = c.shape
    n = w_ada.shape[1]
    tn = d
    return pl.pallas_call(
        _ada_kernel,
        grid=(n // tn,),
        in_specs=[
            pl.BlockSpec((batch, d), lambda j: (0, 0)),
            pl.BlockSpec((d, tn), lambda j: (0, j)),
            pl.BlockSpec((1, tn), lambda j: (0, j)),
        ],
        out_specs=pl.BlockSpec((batch, tn), lambda j: (0, j)),
        out_shape=jax.ShapeDtypeStruct((batch, n), F32),
        name="ada",
    )(c, w_ada, b_ada.reshape(1, n))


def _inproj_kernel(*refs, seq, row_chunk):
    n_slab = D_MODEL // LANES_V7X
    x_refs = refs[:n_slab]
    mod_ref, w_ref, o_ref, h_ref, hf_ref = refs[n_slab:]
    n_blocks = seq // BLOCK
    blocks_per_chunk = row_chunk // BLOCK
    n_slots = h_ref.shape[0]
    gi_kept = N_ATT_GROUPS - 2
    prev_dilation = DILATION_GROUPS[gi_kept][1]
    ratio = DILATION_GROUPS[-1][1] // prev_dilation

    def modulate_rows(gi, chunk, slot):
        dilation = DILATION_GROUPS[gi][1]
        blocks_per_class = n_blocks // dilation
        for kk in range(blocks_per_chunk):
            k = chunk * blocks_per_chunk + kk
            r, n = divmod(k, blocks_per_class)
            src = r + n * (BLOCK * dilation)
            for j in range(n_slab):
                lanes = slice(j * LANES_V7X, (j + 1) * LANES_V7X)
                if gi == N_ATT_GROUPS - 1:
                    c, r_prev = divmod(r, prev_dilation)
                    start = r_prev * (seq // prev_dilation) + n * (BLOCK * ratio) + c
                    hv = hf_ref[j, pl.ds(start, BLOCK, stride=ratio), :]
                else:
                    if dilation == 1:
                        xs = x_refs[j][0, src:src + BLOCK, :]
                    else:
                        xs = x_refs[j][0, pl.ds(src, BLOCK, stride=dilation), :]
                    shift = mod_ref[0, :, lanes]
                    scale = mod_ref[0, :, D_MODEL + j * LANES_V7X:D_MODEL + (j + 1) * LANES_V7X]
                    hv = xs * (1.0 + scale) + shift
                    if gi == gi_kept:
                        hf_ref[j, k * BLOCK:(k + 1) * BLOCK, :] = hv
                h_ref[slot, kk * BLOCK:(kk + 1) * BLOCK, lanes] = hv.astype(BF16)

    def project_rows(gi, chunk, slot):
        rows = slice(chunk * row_chunk, (chunk + 1) * row_chunk)
        for part in range(3):
            c0 = part * ATT_WIDTH + gi * GROUP_WIDTH
            proj = jnp.dot(h_ref[slot], w_ref[:, c0:c0 + GROUP_WIDTH], preferred_element_type=F32)
            if part == 0:
                proj = proj * (1.0 / math.sqrt(HEAD_DIM))
            o_ref[gi, 0, rows, part * GROUP_WIDTH:(part + 1) * GROUP_WIDTH] = proj.astype(BF16)

    items = [(gi, chunk) for gi in range(N_ATT_GROUPS) for chunk in range(seq // row_chunk)]
    modulate_rows(*items[0], 0)
    for n, item in enumerate(items):
        if n + 1 < len(items):
            modulate_rows(*items[n + 1], (n + 1) % n_slots)
        project_rows(*item, n % n_slots)


def _inproj(x, mod3, w_in):
    batch, seq, d = x.shape
    n_slab = d // LANES_V7X
    row_chunk = INPROJ_ROW_CHUNK
    x_specs = [
        pl.BlockSpec((1, seq, LANES_V7X), functools.partial(lambda b, j: (b, 0, j), j=j))
        for j in range(n_slab)
    ]
    return pl.pallas_call(
        functools.partial(_inproj_kernel, seq=seq, row_chunk=row_chunk),
        grid=(batch,),
        in_specs=x_specs + [
            pl.BlockSpec((1, 1, N_ADA * d), lambda b: (b, 0, 0)),
            pl.BlockSpec((d, 3 * ATT_WIDTH), lambda b: (0, 0), pipeline_mode=pl.Buffered(1)),
        ],
        out_specs=pl.BlockSpec((N_ATT_GROUPS, 1, seq, 3 * GROUP_WIDTH), lambda b: (0, b, 0, 0)),
        out_shape=jax.ShapeDtypeStruct((N_ATT_GROUPS, batch, seq, 3 * GROUP_WIDTH), BF16),
        scratch_shapes=[pltpu.VMEM((2, row_chunk, d), BF16), pltpu.VMEM((n_slab, seq, LANES_V7X), F32)],
        compiler_params=pltpu.CompilerParams(
            dimension_semantics=("arbitrary",),
            vmem_limit_bytes=SCOPED_VMEM_LIMIT_BYTES_V7X,
        ),
        name="inproj",
    )(*([x] * n_slab), mod3, w_in)


def _attend_kernel(qkv_ref, banded_ref, paired_ref, o_ref, out_s, lse_s, *, seq):
    stat_refs = (out_s, lse_s)
    n_blocks = seq // BLOCK
    n_pairs = HEADS_PER_GROUP // 2
    ones = jnp.ones((2 * BLOCK, LANES_V7X), BF16)

    def softmax_pv(q2, k2, v2e, load_bias):
        rows = q2.shape[0]
        low_half = lax.broadcasted_iota(jnp.int32, (rows, LANES_V7X), 1) < HEAD_DIM
        accs, dens, maxes = [], [], []
        for hh in range(2):
            keep = low_half if hh == 0 else jnp.logical_not(low_half)
            qm = jnp.where(keep, q2, jnp.zeros_like(q2))
            s = lax.dot_general(qm, k2, (((1,), (1,)), ((), ())), preferred_element_type=F32)
            s = s + load_bias(hh)
            m = jnp.max(s, axis=-1, keepdims=True)
            p = jnp.exp(s - m).astype(BF16)
            o = jnp.dot(p, v2e, preferred_element_type=F32)
            accs.append(o[:, :LANES_V7X])
            dens.append(o[:, LANES_V7X:])
            maxes.append(jnp.broadcast_to(m, (rows, LANES_V7X)))
        acc = jnp.where(low_half, accs[0], accs[1])
        den = jnp.where(low_half, dens[0], dens[1])
        mx = jnp.where(low_half, maxes[0], maxes[1])
        return acc / den, mx + jnp.log(den)

    def put(gi, pair, idx, stats, rows=slice(None)):
        for ref, value in zip(stat_refs, stats):
            ref[gi, pair, idx, :] = value[rows]

    def cols(part, pair):
        c0 = part * GROUP_WIDTH + pair * LANES_V7X
        return slice(c0, c0 + LANES_V7X)

    def banded_loop(gi, dilation, table0):
        blocks_per_class = n_blocks // dilation

        def body(j, carry):
            row0 = pl.multiple_of(j * BLOCK, BLOCK)
            n = j % blocks_per_class
            r = j // blocks_per_class
            first = jnp.where(n == 0, 1, 0)
            prow0 = pl.multiple_of(jnp.maximum(j - 1, 0) * BLOCK, BLOCK)
            nat0 = r + n * (BLOCK * dilation)
            for pair in range(n_pairs):
                q2 = qkv_ref[gi, 0,pl.ds(row0, BLOCK), cols(0, pair)]
                k2 = jnp.concatenate([qkv_ref[gi, 0,pl.ds(prow0, BLOCK), cols(1, pair)],
                                      qkv_ref[gi, 0,pl.ds(row0, BLOCK), cols(1, pair)]], axis=0)
                v2 = jnp.concatenate([qkv_ref[gi, 0,pl.ds(prow0, BLOCK), cols(2, pair)],
                                      qkv_ref[gi, 0,pl.ds(row0, BLOCK), cols(2, pair)]], axis=0)
                v2e = jnp.concatenate([v2, ones], axis=1)
                stats = softmax_pv(
                    q2, k2, v2e, lambda hh: banded_ref[table0 + (pair * 2 + hh) * 2 + first])
                if dilation == 1:
                    put(gi, pair, pl.ds(row0, BLOCK), stats)
                else:
                    put(gi, pair, pl.ds(nat0, BLOCK, stride=dilation), stats)
            return carry

        lax.fori_loop(0, n_blocks, body, 0, unroll=16)

    def paired_loop(gi, dilation, table0):
        def body(j, carry):
            row0 = pl.multiple_of(j * (2 * BLOCK), 2 * BLOCK)
            for pair in range(n_pairs):
                q2 = qkv_ref[gi, 0,pl.ds(row0, 2 * BLOCK), cols(0, pair)]
                k2 = qkv_ref[gi, 0,pl.ds(row0, 2 * BLOCK), cols(1, pair)]
                v2e = jnp.concatenate([qkv_ref[gi, 0,pl.ds(row0, 2 * BLOCK), cols(2, pair)], ones], axis=1)
                stats = softmax_pv(q2, k2, v2e, lambda hh: paired_ref[table0 + pair * 2 + hh])
                for half in range(2):
                    put(gi, pair, pl.ds(2 * j + half, BLOCK, stride=dilation), stats,
                        rows=slice(half * BLOCK, (half + 1) * BLOCK))
            return carry

        lax.fori_loop(0, n_blocks // 2, body, 0, unroll=8)

    n_banded = n_paired = 0
    for gi, (_, dilation) in enumerate(DILATION_GROUPS):
        if _blocks_per_class(dilation) > 1:
            banded_loop(gi, dilation, n_banded)
            n_banded += 2 * HEADS_PER_GROUP
        else:
            paired_loop(gi, dilation, n_paired)
            n_paired += HEADS_PER_GROUP

    rows_per_step = MERGE_ROWS
    for pair in range(n_pairs):
        for c in range(seq // rows_per_step):
            rows = slice(c * rows_per_step, (c + 1) * rows_per_step)
            lses = [lse_s[gi, pair, rows, :] for gi in range(N_ATT_GROUPS)]
            top = functools.reduce(jnp.maximum, lses)
            es = [jnp.exp(l - top) for l in lses]
            den = functools.reduce(lambda a, b: a + b, es)
            num = functools.reduce(
                lambda a, b: a + b, [es[gi] * out_s[gi, pair, rows, :] for gi in range(N_ATT_GROUPS)])
            o_ref[0, rows, pair * LANES_V7X:(pair + 1) * LANES_V7X] = (num / den).astype(BF16)


def _attend(qkv, banded, paired):
    _, batch, seq, width = qkv.shape
    n_pairs = HEADS_PER_GROUP // 2
    return pl.pallas_call(
        functools.partial(_attend_kernel, seq=seq),
        grid=(batch,),
        in_specs=[
            pl.BlockSpec((N_ATT_GROUPS, 1, seq, width), lambda b: (0, b, 0, 0)),
            pl.BlockSpec(banded.shape, lambda b: (0, 0, 0)),
            pl.BlockSpec(paired.shape, lambda b: (0, 0, 0)),
        ],
        out_specs=pl.BlockSpec((1, seq, GROUP_WIDTH), lambda b: (b, 0, 0)),
        out_shape=jax.ShapeDtypeStruct((batch, seq, GROUP_WIDTH), BF16),
        scratch_shapes=[pltpu.VMEM((N_ATT_GROUPS, n_pairs, seq, LANES_V7X), F32)] * 2,
        compiler_params=pltpu.CompilerParams(
            dimension_semantics=("arbitrary",),
            vmem_limit_bytes=SCOPED_VMEM_LIMIT_BYTES_V7X,
        ),
        name="attend",
    )(qkv, banded, paired)


def _mixer_kernel(x_ref, xh_ref, mod_ref, att_ref, wp_ref, wpg_ref, ps_ref, wbp_ref, wba_ref, wout_ref,
                  lg_ref, lb_ref, o_ref, u_s, *, tm, n_sub):
    i = pl.program_id(1)
    d = D_MODEL
    sub = tm // n_sub
    shift = mod_ref[0, :, 0:d]
    scale = mod_ref[0, :, d:2 * d]
    gate_c = mod_ref[0, :, 2 * d:3 * d]

    hh = (xh_ref[0] * (1.0 + scale) + shift).astype(BF16)
    pool_cols = slice(3 * ATT_WIDTH, 3 * ATT_WIDTH + d)
    gate_cols = slice(3 * ATT_WIDTH + d, 3 * ATT_WIDTH + 3 * d)
    u_halo = jnp.dot(hh, wp_ref[:, pool_cols], preferred_element_type=F32)
    u_s[0:POOL_HALO, :] = jnp.where(i > 0, u_halo, jnp.zeros_like(u_halo))

    for s in range(n_sub):
        rows = slice(s * sub, (s + 1) * sub)
        base = POOL_HALO + s * sub
        x = x_ref[0, rows, :]
        h = (x * (1.0 + scale) + shift).astype(BF16)
        u_s[base:base + sub, :] = jnp.dot(h, wp_ref[:, pool_cols], preferred_element_type=F32)
        gates = jnp.dot(h, wp_ref[:, gate_cols], preferred_element_type=F32)

        pos = i * tm + s * sub + lax.broadcasted_iota(jnp.int32, (sub, 1), 0)
        pgs = []
        for gi, w in enumerate(POOL_WINDOWS):
            cols = slice(gi * POOL_GROUP_WIDTH, (gi + 1) * POOL_GROUP_WIDTH)
            u_ext = u_s[base - POOL_HALO:base + sub, cols]
            total = u_ext
            reach = 1
            while reach < w:
                total = total + pltpu.roll(total, reach, axis=0)
                reach *= 2
            u = u_ext[POOL_HALO:]
            total = total[POOL_HALO:]
            inv_count = 1.0 / jnp.minimum(pos + 1, w).astype(F32)
            pm = total * inv_count - u
            pg = jnp.dot(pm.astype(BF16), wpg_ref[gi], preferred_element_type=F32) * ps_ref[:, cols]
            pgs.append(pg.astype(BF16))
        branch_b = jnp.dot(jnp.concatenate(pgs, axis=1), wbp_ref[...], preferred_element_type=F32)

        branch_a = jnp.dot(att_ref[0, rows, :], wba_ref[...], preferred_element_type=F32)
        merged = _sigmoid(gates[:, :d]) * branch_a + _sigmoid(gates[:, d:]) * branch_b
        mixer_out = jnp.dot(merged.astype(BF16), wout_ref[...], preferred_element_type=F32)
        o_ref[0, rows, :] = _layer_norm(DEEPNORM_ALPHA * x + gate_c * mixer_out, lg_ref[...], lb_ref[...])


def _const_spec(shape):
    return pl.BlockSpec(shape, lambda b, i: (0,) * len(shape))


def _mixer(x, mod3, att, wp, wpg, pool_scale, wbp, wba, wout, ln_g, ln_b, *, tm, n_sub):
    batch, seq, d = x.shape
    halo_blocks_per_tile = tm // POOL_HALO
    return pl.pallas_call(
        functools.partial(_mixer_kernel, tm=tm, n_sub=n_sub),
        grid=(batch, seq // tm),
        in_specs=[
            pl.BlockSpec((1, tm, d), lambda b, i: (b, i, 0)),
            pl.BlockSpec((1, POOL_HALO, d), lambda b, i: (b, jnp.maximum(i * halo_blocks_per_tile - 1, 0), 0)),
            pl.BlockSpec((1, 1, N_ADA * d), lambda b, i: (b, 0, 0)),
            pl.BlockSpec((1, tm, GROUP_WIDTH), lambda b, i: (b, i, 0)),
            _const_spec(wp.shape), _const_spec(wpg.shape), _const_spec(pool_scale.shape),
            _const_spec(wbp.shape), _const_spec(wba.shape), _const_spec(wout.shape),
            _const_spec(ln_g.shape), _const_spec(ln_b.shape),
        ],
        out_specs=pl.BlockSpec((1, tm, d), lambda b, i: (b, i, 0)),
        out_shape=jax.ShapeDtypeStruct((batch, seq, d), F32),
        scratch_shapes=[pltpu.VMEM((POOL_HALO + tm, d), F32)],
        compiler_params=pltpu.CompilerParams(
            dimension_semantics=("arbitrary", "arbitrary"),
            vmem_limit_bytes=SCOPED_VMEM_LIMIT_BYTES_V7X,
        ),
        name="mixer",
    )(x, x, mod3, att, wp, wpg, pool_scale, wbp, wba, wout, ln_g, ln_b)


def _ffn_kernel(x_ref, mod_ref, wg_ref, wu_ref, wd_ref, lg_ref, lb_ref, o_ref, *, n_sub):
    d = D_MODEL
    shift = mod_ref[0, :, 3 * d:4 * d]
    scale = mod_ref[0, :, 4 * d:5 * d]
    gate_c = mod_ref[0, :, 5 * d:6 * d]
    sub = x_ref.shape[1] // n_sub
    for s in range(n_sub):
        rows = slice(s * sub, (s + 1) * sub)
        x = x_ref[0, rows, :]
        h = (x * (1.0 + scale) + shift).astype(BF16)
        gt = jnp.dot(h, wg_ref[...], preferred_element_type=F32)
        up = jnp.dot(h, wu_ref[...], preferred_element_type=F32)
        act = (gt * _sigmoid(gt) * up).astype(BF16)
        ffn = jnp.dot(act, wd_ref[...], preferred_element_type=F32)
        o_ref[0, rows, :] = _layer_norm(DEEPNORM_ALPHA * x + gate_c * ffn, lg_ref[...], lb_ref[...])


def _ffn(x1, mod3, wg, wu, wd, ln_g, ln_b, *, tm, n_sub):
    batch, seq, d = x1.shape
    return pl.pallas_call(
        functools.partial(_ffn_kernel, n_sub=n_sub),
        grid=(batch, seq // tm),
        in_specs=[
            pl.BlockSpec((1, tm, d), lambda b, i: (b, i, 0)),
            pl.BlockSpec((1, 1, N_ADA * d), lambda b, i: (b, 0, 0)),
            _const_spec(wg.shape), _const_spec(wu.shape), _const_spec(wd.shape),
            _const_spec(ln_g.shape), _const_spec(ln_b.shape),
        ],
        out_specs=pl.BlockSpec((1, tm, d), lambda b, i: (b, i, 0)),
        out_shape=jax.ShapeDtypeStruct((batch, seq, d), F32),
        compiler_params=pltpu.CompilerParams(
            dimension_semantics=("arbitrary", "arbitrary"),
            vmem_limit_bytes=SCOPED_VMEM_LIMIT_BYTES_V7X,
        ),
        name="ffn",
    )(x1, mod3, wg, wu, wd, ln_g, ln_b)


def kernel(x, c, w_ada, b_ada, w_in, w_branch_att, w_pool_group, pool_scale, w_branch_pool, w_out, ln1_g, ln1_b,
           w_gate, w_up, w_down, ln2_g, ln2_b):
    batch, seq, d = x.shape
    assert d == D_MODEL and w_ada.shape[0] == 1, "one layer of width D_MODEL"
    assert seq == BLOCK * DILATION_GROUPS[-1][1], "sequence must be one block per residue class of the widest dilation"
    l = 0
    mod = _ada(c, w_ada[l], b_ada[l])
    mod3 = mod.reshape(batch, 1, N_ADA * d)

    w_in_bf16 = w_in[l].astype(BF16)
    qkv = _inproj(x, mod3, w_in_bf16)
    banded, paired = _bias_tables()
    att = _attend(qkv, jnp.asarray(banded), jnp.asarray(paired))

    x1 = _mixer(
        x, mod3, att,
        w_in_bf16, w_pool_group[l].astype(BF16), pool_scale[l].reshape(1, d),
        w_branch_pool[l].astype(BF16), w_branch_att[l].astype(BF16), w_out[l].astype(BF16),
        ln1_g[l].reshape(1, d), ln1_b[l].reshape(1, d), tm=MIXER_TILE_ROWS, n_sub=MIXER_SUBTILES)
    return _ffn(x1, mod3, w_gate[l].astype(BF16), w_up[l].astype(BF16), w_down[l].astype(BF16),
                ln2_g[l].reshape(1, d), ln2_b[l].reshape(1, d), tm=FFN_TILE_ROWS, n_sub=FFN_SUBTILES)
```

```python
import functools
import math

import jax
import jax.numpy as jnp
import numpy as np
from jax import lax
from jax.experimental import pallas as pl
from jax.experimental.pallas import tpu as pltpu

D_MODEL = 1024
HEAD_DIM = 64
HEADS_PER_GROUP = 4
DILATION_GROUPS = ((128, 1), (512, 4), (2048, 16))
N_ATT_GROUPS = len(DILATION_GROUPS)
N_ATT_HEADS = N_ATT_GROUPS * HEADS_PER_GROUP
ATT_WIDTH = N_ATT_HEADS * HEAD_DIM
GROUP_WIDTH = HEADS_PER_GROUP * HEAD_DIM
BLOCK = 128
POOL_WINDOWS = (2, 4, 8, 16)
POOL_GROUP_WIDTH = D_MODEL // len(POOL_WINDOWS)
POOL_HALO = 16
N_ADA = 6
DEEPNORM_ALPHA = 2.0 ** 0.25
LN_EPS = 1e-5

LANES_V7X = 128
BF16_SUBLANES_V7X = 16
SCOPED_VMEM_LIMIT_BYTES_V7X = 60000 * 1024

INPROJ_ROW_CHUNK = 512
MERGE_ROWS = 256
MIXER_TILE_ROWS, MIXER_SUBTILES = 1024, 4
FFN_TILE_ROWS, FFN_SUBTILES = 1024, 4

BF16 = jnp.bfloat16
F32 = jnp.float32


def _alibi_slopes(n):
    def pow2_slopes(m):
        start = 2.0 ** (-8.0 / m)
        return [start ** (i + 1) for i in range(m)]
    if math.log2(n).is_integer():
        s = pow2_slopes(n)
    else:
        c = 2 ** math.floor(math.log2(n))
        s = pow2_slopes(c) + pow2_slopes(2 * c)[0::2][: n - c]
    return np.array(sorted(s, reverse=True), dtype=np.float32)


def _bias_tables():
    slopes = _alibi_slopes(N_ATT_HEADS).reshape(N_ATT_GROUPS, HEADS_PER_GROUP)
    qi = np.arange(BLOCK) + BLOCK
    kj = np.arange(2 * BLOCK)
    diff = qi[:, None] - kj[None, :]
    banded, paired = [], []
    for g, (window, dilation) in enumerate(DILATION_GROUPS):
        steps = window // dilation
        valid = (diff >= 0) & (diff <= steps)
        dist = (diff * dilation).astype(np.float32)
        for h in range(HEADS_PER_GROUP):
            bias = -slopes[g, h] * dist
            normal = np.where(valid, bias, -np.inf).astype(np.float32)
            first = np.where(valid & (kj[None, :] >= BLOCK), bias, -np.inf).astype(np.float32)
            if _blocks_per_class(dilation) > 1:
                banded += [normal, first]
            else:
                own = normal[:, BLOCK:]
                off = np.full_like(own, -np.inf)
                paired.append(np.block([[own, off], [off, own]]))
    return np.stack(banded), np.stack(paired)


def _blocks_per_class(dilation):
    return DILATION_GROUPS[-1][1] // dilation


def _layer_norm(y, g, b):
    mu = jnp.mean(y, axis=-1, keepdims=True)
    yc = y - mu
    var = jnp.mean(yc * yc, axis=-1, keepdims=True)
    return yc * lax.rsqrt(var + LN_EPS) * g + b


def _sigmoid(x):
    return 1.0 / (1.0 + jnp.exp(-x))


def _ada_kernel(c_ref, w_ref, b_ref, o_ref):
    c = c_ref[...]
    s = (c * _sigmoid(c)).astype(BF16)
    o_ref[...] = jnp.dot(s, w_ref[...].astype(BF16), preferred_element_type=F32) + b_ref[...]


def _ada(c, w_ada, b_ada):
    batch, d = c.shape
    n = w_ada.shape[1]
    tn = d
    return pl.pallas_call(
        _ada_kernel,
        grid=(n // tn,),
        in_specs=[
            pl.BlockSpec((batch, d), lambda j: (0, 0)),
            pl.BlockSpec((d, tn), lambda j: (0, j)),
            pl.BlockSpec((1, tn), lambda j: (0, j)),
        ],
        out_specs=pl.BlockSpec((batch, tn), lambda j: (0, j)),
        out_shape=jax.ShapeDtypeStruct((batch, n), F32),
        name="ada",
    )(c, w_ada, b_ada.reshape(1, n))


def _inproj_kernel(*refs, seq, row_chunk):
    n_slab = D_MODEL // LANES_V7X
    x_refs = refs[:n_slab]
    mod_ref, w_ref, o_ref, h_ref, hf_ref = refs[n_slab:]
    n_blocks = seq // BLOCK
    blocks_per_chunk = row_chunk // BLOCK
    n_slots = h_ref.shape[0]
    gi_kept = N_ATT_GROUPS - 2
    prev_dilation = DILATION_GROUPS[gi_kept][1]
    ratio = DILATION_GROUPS[-1][1] // prev_dilation

    def modulate_rows(gi, chunk, slot):
        dilation = DILATION_GROUPS[gi][1]
        blocks_per_class = n_blocks // dilation
        for kk in range(blocks_per_chunk):
            k = chunk * blocks_per_chunk + kk
            r, n = divmod(k, blocks_per_class)
            src = r + n * (BLOCK * dilation)
            for j in range(n_slab):
                lanes = slice(j * LANES_V7X, (j + 1) * LANES_V7X)
                if gi == N_ATT_GROUPS - 1:
                    c, r_prev = divmod(r, prev_dilation)
                    start = r_prev * (seq // prev_dilation) + n * (BLOCK * ratio) + c
                    hv = hf_ref[j, pl.ds(start, BLOCK, stride=ratio), :]
                else:
                    if dilation == 1:
                        xs = x_refs[j][0, src:src + BLOCK, :]
                    else:
                        xs = x_refs[j][0, pl.ds(src, BLOCK, stride=dilation), :]
                    shift = mod_ref[0, :, lanes]
                    scale = mod_ref[0, :, D_MODEL + j * LANES_V7X:D_MODEL + (j + 1) * LANES_V7X]
                    hv = xs * (1.0 + scale) + shift
                    if gi == gi_kept:
                        hf_ref[j, k * BLOCK:(k + 1) * BLOCK, :] = hv
                h_ref[slot, kk * BLOCK:(kk + 1) * BLOCK, lanes] = hv.astype(BF16)

    def project_rows(gi, chunk, slot):
        rows = slice(chunk * row_chunk, (chunk + 1) * row_chunk)
        for part in range(3):
            c0 = part * ATT_WIDTH + gi * GROUP_WIDTH
            proj = jnp.dot(h_ref[slot], w_ref[:, c0:c0 + GROUP_WIDTH], preferred_element_type=F32)
            if part == 0:
                proj = proj * (1.0 / math.sqrt(HEAD_DIM))
            o_ref[gi, 0, rows, part * GROUP_WIDTH:(part + 1) * GROUP_WIDTH] = proj.astype(BF16)

    items = [(gi, chunk) for gi in range(N_ATT_GROUPS) for chunk in range(seq // row_chunk)]
    modulate_rows(*items[0], 0)
    for n, item in enumerate(items):
        if n + 1 < len(items):
            modulate_rows(*items[n + 1], (n + 1) % n_slots)
        project_rows(*item, n % n_slots)


def _inproj(x, mod3, w_in):
    batch, seq, d = x.shape
    n_slab = d // LANES_V7X
    row_chunk = INPROJ_ROW_CHUNK
    x_specs = [
        pl.BlockSpec((1, seq, LANES_V7X), functools.partial(lambda b, j: (b, 0, j), j=j))
        for j in range(n_slab)
    ]
    return pl.pallas_call(
        functools.partial(_inproj_kernel, seq=seq, row_chunk=row_chunk),
        grid=(batch,),
        in_specs=x_specs + [
            pl.BlockSpec((1, 1, N_ADA * d), lambda b: (b, 0, 0)),
            pl.BlockSpec((d, 3 * ATT_WIDTH), lambda b: (0, 0), pipeline_mode=pl.Buffered(1)),
        ],
        out_specs=pl.BlockSpec((N_ATT_GROUPS, 1, seq, 3 * GROUP_WIDTH), lambda b: (0, b, 0, 0)),
        out_shape=jax.ShapeDtypeStruct((N_ATT_GROUPS, batch, seq, 3 * GROUP_WIDTH), BF16),
        scratch_shapes=[pltpu.VMEM((2, row_chunk, d), BF16), pltpu.VMEM((n_slab, seq, LANES_V7X), F32)],
        compiler_params=pltpu.CompilerParams(
            dimension_semantics=("arbitrary",),
            vmem_limit_bytes=SCOPED_VMEM_LIMIT_BYTES_V7X,
        ),
        name="inproj",
    )(*([x] * n_slab), mod3, w_in)


def _attend_kernel(*refs, seq, n_cast):
    qkv_ref, banded_ref, paired_ref = refs[:3]
    cast_src_refs = refs[3:3 + n_cast]
    o_ref = refs[3 + n_cast]
    cast_dst_refs = refs[4 + n_cast:4 + 2 * n_cast]
    out_s, lse_s = refs[4 + 2 * n_cast:]
    stat_refs = (out_s, lse_s)

    for src, dst in zip(cast_src_refs, cast_dst_refs):
        dst[...] = src[...].astype(BF16)

    n_blocks = seq // BLOCK
    n_pairs = HEADS_PER_GROUP // 2
    ones = jnp.ones((2 * BLOCK, LANES_V7X), BF16)

    def softmax_pv(q2, k2, v2e, load_bias):
        rows = q2.shape[0]
        low_half = lax.broadcasted_iota(jnp.int32, (rows, LANES_V7X), 1) < HEAD_DIM
        accs, dens, maxes = [], [], []
        for hh in range(2):
            keep = low_half if hh == 0 else jnp.logical_not(low_half)
            qm = jnp.where(keep, q2, jnp.zeros_like(q2))
            s = lax.dot_general(qm, k2, (((1,), (1,)), ((), ())), preferred_element_type=F32)
            s = s + load_bias(hh)
            m = jnp.max(s, axis=-1, keepdims=True)
            p = jnp.exp(s - m).astype(BF16)
            o = jnp.dot(p, v2e, preferred_element_type=F32)
            accs.append(o[:, :LANES_V7X])
            dens.append(o[:, LANES_V7X:])
            maxes.append(jnp.broadcast_to(m, (rows, LANES_V7X)))
        acc = jnp.where(low_half, accs[0], accs[1])
        den = jnp.where(low_half, dens[0], dens[1])
        mx = jnp.where(low_half, maxes[0], maxes[1])
        return acc / den, mx + jnp.log(den)

    def put(gi, pair, idx, stats, rows=slice(None)):
        for ref, value in zip(stat_refs, stats):
            ref[gi, pair, idx, :] = value[rows]

    def cols(part, pair):
        c0 = part * GROUP_WIDTH + pair * LANES_V7X
        return slice(c0, c0 + LANES_V7X)

    def banded_loop(gi, dilation, table0):
        blocks_per_class = n_blocks // dilation

        def body(j, carry):
            row0 = pl.multiple_of(j * BLOCK, BLOCK)
            n = j % blocks_per_class
            r = j // blocks_per_class
            first = jnp.where(n == 0, 1, 0)
            prow0 = pl.multiple_of(jnp.maximum(j - 1, 0) * BLOCK, BLOCK)
            nat0 = r + n * (BLOCK * dilation)
            for pair in range(n_pairs):
                q2 = qkv_ref[gi, 0,pl.ds(row0, BLOCK), cols(0, pair)]
                k2 = jnp.concatenate([qkv_ref[gi, 0,pl.ds(prow0, BLOCK), cols(1, pair)],
                                      qkv_ref[gi, 0,pl.ds(row0, BLOCK), cols(1, pair)]], axis=0)
                v2 = jnp.concatenate([qkv_ref[gi, 0,pl.ds(prow0, BLOCK), cols(2, pair)],
                                      qkv_ref[gi, 0,pl.ds(row0, BLOCK), cols(2, pair)]], axis=0)
                v2e = jnp.concatenate([v2, ones], axis=1)
                stats = softmax_pv(
                    q2, k2, v2e, lambda hh: banded_ref[table0 + (pair * 2 + hh) * 2 + first])
                if dilation == 1:
                    put(gi, pair, pl.ds(row0, BLOCK), stats)
                else:
                    put(gi, pair, pl.ds(nat0, BLOCK, stride=dilation), stats)
            return carry

        lax.fori_loop(0, n_blocks, body, 0, unroll=16)

    def paired_loop(gi, dilation, table0):
        def body(j, carry):
            row0 = pl.multiple_of(j * (2 * BLOCK), 2 * BLOCK)
            for pair in range(n_pairs):
                q2 = qkv_ref[gi, 0,pl.ds(row0, 2 * BLOCK), cols(0, pair)]
                k2 = qkv_ref[gi, 0,pl.ds(row0, 2 * BLOCK), cols(1, pair)]
                v2e = jnp.concatenate([qkv_ref[gi, 0,pl.ds(row0, 2 * BLOCK), cols(2, pair)], ones], axis=1)
                stats = softmax_pv(q2, k2, v2e, lambda hh: paired_ref[table0 + pair * 2 + hh])
                for half in range(2):
                    put(gi, pair, pl.ds(2 * j + half, BLOCK, stride=dilation), stats,
                        rows=slice(half * BLOCK, (half + 1) * BLOCK))
            return carry

        lax.fori_loop(0, n_blocks // 2, body, 0, unroll=8)

    n_banded = n_paired = 0
    for gi, (_, dilation) in enumerate(DILATION_GROUPS):
        if _blocks_per_class(dilation) > 1:
            banded_loop(gi, dilation, n_banded)
            n_banded += 2 * HEADS_PER_GROUP
        else:
            paired_loop(gi, dilation, n_paired)
            n_paired += HEADS_PER_GROUP

    rows_per_step = MERGE_ROWS
    for pair in range(n_pairs):
        for c in range(seq // rows_per_step):
            rows = slice(c * rows_per_step, (c + 1) * rows_per_step)
            lses = [lse_s[gi, pair, rows, :] for gi in range(N_ATT_GROUPS)]
            top = functools.reduce(jnp.maximum, lses)
            es = [jnp.exp(l - top) for l in lses]
            den = functools.reduce(lambda a, b: a + b, es)
            num = functools.reduce(
                lambda a, b: a + b, [es[gi] * out_s[gi, pair, rows, :] for gi in range(N_ATT_GROUPS)])
            o_ref[0, rows, pair * LANES_V7X:(pair + 1) * LANES_V7X] = (num / den).astype(BF16)


def _attend(qkv, banded, paired, weights):
    _, batch, seq, width = qkv.shape
    n_pairs = HEADS_PER_GROUP // 2

    def row_slice_spec(w):
        rows = w.shape[-2]
        assert rows % (batch * BF16_SUBLANES_V7X) == 0, "row slices must be whole bf16 tiles"
        lead = w.ndim - 2
        return pl.BlockSpec(w.shape[:-2] + (rows // batch, w.shape[-1]), lambda b: (0,) * lead + (b, 0))

    cast_specs = [row_slice_spec(w) for w in weights]
    results = pl.pallas_call(
        functools.partial(_attend_kernel, seq=seq, n_cast=len(weights)),
        grid=(batch,),
        in_specs=[
            pl.BlockSpec((N_ATT_GROUPS, 1, seq, width), lambda b: (0, b, 0, 0)),
            pl.BlockSpec(banded.shape, lambda b: (0, 0, 0)),
            pl.BlockSpec(paired.shape, lambda b: (0, 0, 0)),
        ] + cast_specs,
        out_specs=[pl.BlockSpec((1, seq, GROUP_WIDTH), lambda b: (b, 0, 0))]
        + [row_slice_spec(w) for w in weights],
        out_shape=[jax.ShapeDtypeStruct((batch, seq, GROUP_WIDTH), BF16)]
        + [jax.ShapeDtypeStruct(w.shape, BF16) for w in weights],
        scratch_shapes=[pltpu.VMEM((N_ATT_GROUPS, n_pairs, seq, LANES_V7X), F32)] * 2,
        compiler_params=pltpu.CompilerParams(
            dimension_semantics=("arbitrary",),
            vmem_limit_bytes=SCOPED_VMEM_LIMIT_BYTES_V7X,
        ),
        name="attend",
    )(qkv, banded, paired, *weights)
    return results[0], results[1:]


def _mixer_kernel(x_ref, xh_ref, mod_ref, att_ref, wp_ref, wpg_ref, ps_ref, wbp_ref, wba_ref, wout_ref,
                  lg_ref, lb_ref, o_ref, u_s, *, tm, n_sub):
    i = pl.program_id(1)
    d = D_MODEL
    sub = tm // n_sub
    shift = mod_ref[0, :, 0:d]
    scale = mod_ref[0, :, d:2 * d]
    gate_c = mod_ref[0, :, 2 * d:3 * d]

    hh = (xh_ref[0] * (1.0 + scale) + shift).astype(BF16)
    pool_cols = slice(3 * ATT_WIDTH, 3 * ATT_WIDTH + d)
    gate_cols = slice(3 * ATT_WIDTH + d, 3 * ATT_WIDTH + 3 * d)
    u_halo = jnp.dot(hh, wp_ref[:, pool_cols], preferred_element_type=F32)
    u_s[0:POOL_HALO, :] = jnp.where(i > 0, u_halo, jnp.zeros_like(u_halo))

    for s in range(n_sub):
        rows = slice(s * sub, (s + 1) * sub)
        base = POOL_HALO + s * sub
        x = x_ref[0, rows, :]
        h = (x * (1.0 + scale) + shift).astype(BF16)
        u_s[base:base + sub, :] = jnp.dot(h, wp_ref[:, pool_cols], preferred_element_type=F32)
        gates = jnp.dot(h, wp_ref[:, gate_cols], preferred_element_type=F32)

        pos = i * tm + s * sub + lax.broadcasted_iota(jnp.int32, (sub, 1), 0)
        pgs = []
        for gi, w in enumerate(POOL_WINDOWS):
            cols = slice(gi * POOL_GROUP_WIDTH, (gi + 1) * POOL_GROUP_WIDTH)
            u_ext = u_s[base - POOL_HALO:base + sub, cols]
            total = u_ext
            reach = 1
            while reach < w:
                total = total + pltpu.roll(total, reach, axis=0)
                reach *= 2
            u = u_ext[POOL_HALO:]
            total = total[POOL_HALO:]
            inv_count = 1.0 / jnp.minimum(pos + 1, w).astype(F32)
            pm = total * inv_count - u
            pg = jnp.dot(pm.astype(BF16), wpg_ref[gi], preferred_element_type=F32) * ps_ref[:, cols]
            pgs.append(pg.astype(BF16))
        branch_b = jnp.dot(jnp.concatenate(pgs, axis=1), wbp_ref[...], preferred_element_type=F32)

        branch_a = jnp.dot(att_ref[0, rows, :], wba_ref[...], preferred_element_type=F32)
        merged = _sigmoid(gates[:, :d]) * branch_a + _sigmoid(gates[:, d:]) * branch_b
        mixer_out = jnp.dot(merged.astype(BF16), wout_ref[...], preferred_element_type=F32)
        o_ref[0, rows, :] = _layer_norm(DEEPNORM_ALPHA * x + gate_c * mixer_out, lg_ref[...], lb_ref[...])


def _const_spec(shape):
    return pl.BlockSpec(shape, lambda b, i: (0,) * len(shape))


def _mixer(x, mod3, att, wp, wpg, pool_scale, wbp, wba, wout, ln_g, ln_b, *, tm, n_sub):
    batch, seq, d = x.shape
    halo_blocks_per_tile = tm // POOL_HALO
    return pl.pallas_call(
        functools.partial(_mixer_kernel, tm=tm, n_sub=n_sub),
        grid=(batch, seq // tm),
        in_specs=[
            pl.BlockSpec((1, tm, d), lambda b, i: (b, i, 0)),
            pl.BlockSpec((1, POOL_HALO, d), lambda b, i: (b, jnp.maximum(i * halo_blocks_per_tile - 1, 0), 0)),
            pl.BlockSpec((1, 1, N_ADA * d), lambda b, i: (b, 0, 0)),
            pl.BlockSpec((1, tm, GROUP_WIDTH), lambda b, i: (b, i, 0)),
            _const_spec(wp.shape), _const_spec(wpg.shape), _const_spec(pool_scale.shape),
            _const_spec(wbp.shape), _const_spec(wba.shape), _const_spec(wout.shape),
            _const_spec(ln_g.shape), _const_spec(ln_b.shape),
        ],
        out_specs=pl.BlockSpec((1, tm, d), lambda b, i: (b, i, 0)),
        out_shape=jax.ShapeDtypeStruct((batch, seq, d), F32),
        scratch_shapes=[pltpu.VMEM((POOL_HALO + tm, d), F32)],
        compiler_params=pltpu.CompilerParams(
            dimension_semantics=("arbitrary", "arbitrary"),
            vmem_limit_bytes=SCOPED_VMEM_LIMIT_BYTES_V7X,
        ),
        name="mixer",
    )(x, x, mod3, att, wp, wpg, pool_scale, wbp, wba, wout, ln_g, ln_b)


def _ffn_kernel(x_ref, mod_ref, wg_ref, wu_ref, wd_ref, lg_ref, lb_ref, o_ref, *, n_sub):
    d = D_MODEL
    shift = mod_ref[0, :, 3 * d:4 * d]
    scale = mod_ref[0, :, 4 * d:5 * d]
    gate_c = mod_ref[0, :, 5 * d:6 * d]
    sub = x_ref.shape[1] // n_sub
    for s in range(n_sub):
        rows = slice(s * sub, (s + 1) * sub)
        x = x_ref[0, rows, :]
        h = (x * (1.0 + scale) + shift).astype(BF16)
        gt = jnp.dot(h, wg_ref[...], preferred_element_type=F32)
        up = jnp.dot(h, wu_ref[...], preferred_element_type=F32)
        act = (gt * _sigmoid(gt) * up).astype(BF16)
        ffn = jnp.dot(act, wd_ref[...], preferred_element_type=F32)
        o_ref[0, rows, :] = _layer_norm(DEEPNORM_ALPHA * x + gate_c * ffn, lg_ref[...], lb_ref[...])


def _ffn(x1, mod3, wg, wu, wd, ln_g, ln_b, *, tm, n_sub):
    batch, seq, d = x1.shape
    return pl.pallas_call(
        functools.partial(_ffn_kernel, n_sub=n_sub),
        grid=(batch, seq // tm),
        in_specs=[
            pl.BlockSpec((1, tm, d), lambda b, i: (b, i, 0)),
            pl.BlockSpec((1, 1, N_ADA * d), lambda b, i: (b, 0, 0)),
            _const_spec(wg.shape), _const_spec(wu.shape), _const_spec(wd.shape),
            _const_spec(ln_g.shape), _const_spec(ln_b.shape),
        ],
        out_specs=pl.BlockSpec((1, tm, d), lambda b, i: (b, i, 0)),
        out_shape=jax.ShapeDtypeStruct((batch, seq, d), F32),
        compiler_params=pltpu.CompilerParams(
            dimension_semantics=("arbitrary", "arbitrary"),
            vmem_limit_bytes=SCOPED_VMEM_LIMIT_BYTES_V7X,
        ),
        name="ffn",
    )(x1, mod3, wg, wu, wd, ln_g, ln_b)


def kernel(x, c, w_ada, b_ada, w_in, w_branch_att, w_pool_group, pool_scale, w_branch_pool, w_out, ln1_g, ln1_b,
           w_gate, w_up, w_down, ln2_g, ln2_b):
    batch, seq, d = x.shape
    assert d == D_MODEL and w_ada.shape[0] == 1, "one layer of width D_MODEL"
    assert seq == BLOCK * DILATION_GROUPS[-1][1], "sequence must be one block per residue class of the widest dilation"
    l = 0
    mod = _ada(c, w_ada[l], b_ada[l])
    mod3 = mod.reshape(batch, 1, N_ADA * d)

    w_in_bf16 = w_in[l].astype(BF16)
    qkv = _inproj(x, mod3, w_in_bf16)
    banded, paired = _bias_tables()
    att, (wpg, wbp, wba, wout, wg, wu, wd) = _attend(
        qkv, jnp.asarray(banded), jnp.asarray(paired),
        [w_pool_group[l], w_branch_pool[l], w_branch_att[l], w_out[l], w_gate[l], w_up[l], w_down[l]])

    x1 = _mixer(
        x, mod3, att, w_in_bf16, wpg, pool_scale[l].reshape(1, d), wbp, wba, wout,
        ln1_g[l].reshape(1, d), ln1_b[l].reshape(1, d), tm=MIXER_TILE_ROWS, n_sub=MIXER_SUBTILES)
    return _ffn(x1, mod3, wg, wu, wd,
                ln2_g[l].reshape(1, d), ln2_b[l].reshape(1, d), tm=FFN_TILE_ROWS, n_sub=FFN_SUBTILES)
```

```python
import functools
import math

import jax
import jax.numpy as jnp
import numpy as np
from jax import lax
from jax.experimental import pallas as pl
from jax.experimental.pallas import tpu as pltpu

D_MODEL = 1024
HEAD_DIM = 64
HEADS_PER_GROUP = 4
DILATION_GROUPS = ((128, 1), (512, 4), (2048, 16))
N_ATT_GROUPS = len(DILATION_GROUPS)
N_ATT_HEADS = N_ATT_GROUPS * HEADS_PER_GROUP
ATT_WIDTH = N_ATT_HEADS * HEAD_DIM
GROUP_WIDTH = HEADS_PER_GROUP * HEAD_DIM
BLOCK = 128
POOL_WINDOWS = (2, 4, 8, 16)
POOL_GROUP_WIDTH = D_MODEL // len(POOL_WINDOWS)
POOL_HALO = 16
N_ADA = 6
DEEPNORM_ALPHA = 2.0 ** 0.25
LN_EPS = 1e-5

LANES_V7X = 128
BF16_SUBLANES_V7X = 16
SCOPED_VMEM_LIMIT_BYTES_V7X = 60000 * 1024

INPROJ_ROW_CHUNK = 512
MERGE_ROWS = 256
MIXER_TILE_ROWS, MIXER_SUBTILES = 1024, 4
FFN_TILE_ROWS, FFN_SUBTILES = 1024, 8

BF16 = jnp.bfloat16
F32 = jnp.float32


def _alibi_slopes(n):
    def pow2_slopes(m):
        start = 2.0 ** (-8.0 / m)
        return [start ** (i + 1) for i in range(m)]
    if math.log2(n).is_integer():
        s = pow2_slopes(n)
    else:
        c = 2 ** math.floor(math.log2(n))
        s = pow2_slopes(c) + pow2_slopes(2 * c)[0::2][: n - c]
    return np.array(sorted(s, reverse=True), dtype=np.float32)


def _bias_tables():
    slopes = _alibi_slopes(N_ATT_HEADS).reshape(N_ATT_GROUPS, HEADS_PER_GROUP)
    qi = np.arange(BLOCK) + BLOCK
    kj = np.arange(2 * BLOCK)
    diff = qi[:, None] - kj[None, :]
    banded, paired = [], []
    for g, (window, dilation) in enumerate(DILATION_GROUPS):
        steps = window // dilation
        valid = (diff >= 0) & (diff <= steps)
        dist = (diff * dilation).astype(np.float32)
        for h in range(HEADS_PER_GROUP):
            bias = -slopes[g, h] * dist
            normal = np.where(valid, bias, -np.inf).astype(np.float32)
            first = np.where(valid & (kj[None, :] >= BLOCK), bias, -np.inf).astype(np.float32)
            if _blocks_per_class(dilation) > 1:
                banded += [normal, first]
            else:
                own = normal[:, BLOCK:]
                off = np.full_like(own, -np.inf)
                paired.append(np.block([[own, off], [off, own]]))
    return np.stack(banded), np.stack(paired)


def _blocks_per_class(dilation):
    return DILATION_GROUPS[-1][1] // dilation


def _layer_norm(y, g, b):
    mu = jnp.mean(y, axis=-1, keepdims=True)
    yc = y - mu
    var = jnp.mean(yc * yc, axis=-1, keepdims=True)
    return yc * lax.rsqrt(var + LN_EPS) * g + b


def _sigmoid(x):
    return 1.0 / (1.0 + jnp.exp(-x))


def _ada_kernel(c_ref, w_ref, b_ref, win_ref, o_ref, win_bf16_ref):
    c = c_ref[...]
    s = (c * _sigmoid(c)).astype(BF16)
    o_ref[...] = jnp.dot(s, w_ref[...].astype(BF16), preferred_element_type=F32) + b_ref[...]
    win_bf16_ref[...] = win_ref[...].astype(BF16)


def _ada(c, w_ada, b_ada, w_in):
    batch, d = c.shape
    n = w_ada.shape[1]
    tn = d
    steps = n // tn
    win_cols = w_in.shape[1] // steps
    assert win_cols * steps == w_in.shape[1] and win_cols % LANES_V7X == 0
    return pl.pallas_call(
        _ada_kernel,
        grid=(steps,),
        in_specs=[
            pl.BlockSpec((batch, d), lambda j: (0, 0)),
            pl.BlockSpec((d, tn), lambda j: (0, j)),
            pl.BlockSpec((1, tn), lambda j: (0, j)),
            pl.BlockSpec((w_in.shape[0], win_cols), lambda j: (0, j)),
        ],
        out_specs=[
            pl.BlockSpec((batch, tn), lambda j: (0, j)),
            pl.BlockSpec((w_in.shape[0], win_cols), lambda j: (0, j)),
        ],
        out_shape=[jax.ShapeDtypeStruct((batch, n), F32), jax.ShapeDtypeStruct(w_in.shape, BF16)],
        name="ada",
    )(c, w_ada, b_ada.reshape(1, n), w_in)


def _inproj_kernel(*refs, seq, row_chunk):
    n_slab = D_MODEL // LANES_V7X
    x_refs = refs[:n_slab]
    mod_ref, w_ref, o_ref, h_ref, hf_ref = refs[n_slab:]
    n_blocks = seq // BLOCK
    blocks_per_chunk = row_chunk // BLOCK
    n_slots = h_ref.shape[0]
    gi_kept = N_ATT_GROUPS - 2
    prev_dilation = DILATION_GROUPS[gi_kept][1]
    ratio = DILATION_GROUPS[-1][1] // prev_dilation

    def modulate_rows(gi, chunk, slot):
        dilation = DILATION_GROUPS[gi][1]
        blocks_per_class = n_blocks // dilation
        for kk in range(blocks_per_chunk):
            k = chunk * blocks_per_chunk + kk
            r, n = divmod(k, blocks_per_class)
            src = r + n * (BLOCK * dilation)
            for j in range(n_slab):
                lanes = slice(j * LANES_V7X, (j + 1) * LANES_V7X)
                if gi == N_ATT_GROUPS - 1:
                    c, r_prev = divmod(r, prev_dilation)
                    start = r_prev * (seq // prev_dilation) + n * (BLOCK * ratio) + c
                    hv = hf_ref[j, pl.ds(start, BLOCK, stride=ratio), :]
                else:
                    if dilation == 1:
                        xs = x_refs[j][0, src:src + BLOCK, :]
                    else:
                        xs = x_refs[j][0, pl.ds(src, BLOCK, stride=dilation), :]
                    shift = mod_ref[0, :, lanes]
                    scale = mod_ref[0, :, D_MODEL + j * LANES_V7X:D_MODEL + (j + 1) * LANES_V7X]
                    hv = xs * (1.0 + scale) + shift
                    if gi == gi_kept:
                        hf_ref[j, k * BLOCK:(k + 1) * BLOCK, :] = hv
                h_ref[slot, kk * BLOCK:(kk + 1) * BLOCK, lanes] = hv.astype(BF16)

    def project_rows(gi, chunk, slot):
        rows = slice(chunk * row_chunk, (chunk + 1) * row_chunk)
        for part in range(3):
            c0 = part * ATT_WIDTH + gi * GROUP_WIDTH
            proj = jnp.dot(h_ref[slot], w_ref[:, c0:c0 + GROUP_WIDTH], preferred_element_type=F32)
            if part == 0:
                proj = proj * (1.0 / math.sqrt(HEAD_DIM))
            o_ref[gi, 0, rows, part * GROUP_WIDTH:(part + 1) * GROUP_WIDTH] = proj.astype(BF16)

    items = [(gi, chunk) for gi in range(N_ATT_GROUPS) for chunk in range(seq // row_chunk)]
    modulate_rows(*items[0], 0)
    for n, item in enumerate(items):
        if n + 1 < len(items):
            modulate_rows(*items[n + 1], (n + 1) % n_slots)
        project_rows(*item, n % n_slots)


def _inproj(x, mod3, w_in):
    batch, seq, d = x.shape
    n_slab = d // LANES_V7X
    row_chunk = INPROJ_ROW_CHUNK
    x_specs = [
        pl.BlockSpec((1, seq, LANES_V7X), functools.partial(lambda b, j: (b, 0, j), j=j))
        for j in range(n_slab)
    ]
    return pl.pallas_call(
        functools.partial(_inproj_kernel, seq=seq, row_chunk=row_chunk),
        grid=(batch,),
        in_specs=x_specs + [
            pl.BlockSpec((1, 1, N_ADA * d), lambda b: (b, 0, 0)),
            pl.BlockSpec((d, 3 * ATT_WIDTH), lambda b: (0, 0), pipeline_mode=pl.Buffered(1)),
        ],
        out_specs=pl.BlockSpec((N_ATT_GROUPS, 1, seq, 3 * GROUP_WIDTH), lambda b: (0, b, 0, 0)),
        out_shape=jax.ShapeDtypeStruct((N_ATT_GROUPS, batch, seq, 3 * GROUP_WIDTH), BF16),
        scratch_shapes=[pltpu.VMEM((2, row_chunk, d), BF16), pltpu.VMEM((n_slab, seq, LANES_V7X), F32)],
        compiler_params=pltpu.CompilerParams(
            dimension_semantics=("arbitrary",),
            vmem_limit_bytes=SCOPED_VMEM_LIMIT_BYTES_V7X,
        ),
        name="inproj",
    )(*([x] * n_slab), mod3, w_in)


def _attend_kernel(*refs, seq, n_cast):
    qkv_ref, banded_ref, paired_ref = refs[:3]
    cast_src_refs = refs[3:3 + n_cast]
    o_ref = refs[3 + n_cast]
    cast_dst_refs = refs[4 + n_cast:4 + 2 * n_cast]
    out_s, lse_s = refs[4 + 2 * n_cast:]
    stat_refs = (out_s, lse_s)

    for src, dst in zip(cast_src_refs, cast_dst_refs):
        dst[...] = src[...].astype(BF16)

    n_blocks = seq // BLOCK
    n_pairs = HEADS_PER_GROUP // 2
    ones = jnp.ones((2 * BLOCK, LANES_V7X), BF16)

    def softmax_pv(q2, k2, v2e, load_bias):
        rows = q2.shape[0]
        low_half = lax.broadcasted_iota(jnp.int32, (rows, LANES_V7X), 1) < HEAD_DIM
        accs, dens, maxes = [], [], []
        for hh in range(2):
            keep = low_half if hh == 0 else jnp.logical_not(low_half)
            qm = jnp.where(keep, q2, jnp.zeros_like(q2))
            s = lax.dot_general(qm, k2, (((1,), (1,)), ((), ())), preferred_element_type=F32)
            s = s + load_bias(hh)
            m = jnp.max(s, axis=-1, keepdims=True)
            p = jnp.exp(s - m).astype(BF16)
            o = jnp.dot(p, v2e, preferred_element_type=F32)
            accs.append(o[:, :LANES_V7X])
            dens.append(o[:, LANES_V7X:])
            maxes.append(jnp.broadcast_to(m, (rows, LANES_V7X)))
        acc = jnp.where(low_half, accs[0], accs[1])
        den = jnp.where(low_half, dens[0], dens[1])
        mx = jnp.where(low_half, maxes[0], maxes[1])
        return acc / den, mx + jnp.log(den)

    def put(gi, pair, idx, stats, rows=slice(None)):
        for ref, value in zip(stat_refs, stats):
            ref[gi, pair, idx, :] = value[rows]

    def cols(part, pair):
        c0 = part * GROUP_WIDTH + pair * LANES_V7X
        return slice(c0, c0 + LANES_V7X)

    def banded_loop(gi, dilation, table0):
        blocks_per_class = n_blocks // dilation

        def body(j, carry):
            row0 = pl.multiple_of(j * BLOCK, BLOCK)
            n = j % blocks_per_class
            r = j // blocks_per_class
            first = jnp.where(n == 0, 1, 0)
            prow0 = pl.multiple_of(jnp.maximum(j - 1, 0) * BLOCK, BLOCK)
            nat0 = r + n * (BLOCK * dilation)
            for pair in range(n_pairs):
                q2 = qkv_ref[gi, 0,pl.ds(row0, BLOCK), cols(0, pair)]
                k2 = jnp.concatenate([qkv_ref[gi, 0,pl.ds(prow0, BLOCK), cols(1, pair)],
                                      qkv_ref[gi, 0,pl.ds(row0, BLOCK), cols(1, pair)]], axis=0)
                v2 = jnp.concatenate([qkv_ref[gi, 0,pl.ds(prow0, BLOCK), cols(2, pair)],
                                      qkv_ref[gi, 0,pl.ds(row0, BLOCK), cols(2, pair)]], axis=0)
                v2e = jnp.concatenate([v2, ones], axis=1)
                stats = softmax_pv(
                    q2, k2, v2e, lambda hh: banded_ref[table0 + (pair * 2 + hh) * 2 + first])
                if dilation == 1:
                    put(gi, pair, pl.ds(row0, BLOCK), stats)
                else:
                    put(gi, pair, pl.ds(nat0, BLOCK, stride=dilation), stats)
            return carry

        lax.fori_loop(0, n_blocks, body, 0, unroll=16)

    def paired_loop(gi, dilation, table0):
        def body(j, carry):
            row0 = pl.multiple_of(j * (2 * BLOCK), 2 * BLOCK)
            for pair in range(n_pairs):
                q2 = qkv_ref[gi, 0,pl.ds(row0, 2 * BLOCK), cols(0, pair)]
                k2 = qkv_ref[gi, 0,pl.ds(row0, 2 * BLOCK), cols(1, pair)]
                v2e = jnp.concatenate([qkv_ref[gi, 0,pl.ds(row0, 2 * BLOCK), cols(2, pair)], ones], axis=1)
                stats = softmax_pv(q2, k2, v2e, lambda hh: paired_ref[table0 + pair * 2 + hh])
                for half in range(2):
                    put(gi, pair, pl.ds(2 * j + half, BLOCK, stride=dilation), stats,
                        rows=slice(half * BLOCK, (half + 1) * BLOCK))
            return carry

        lax.fori_loop(0, n_blocks // 2, body, 0, unroll=8)

    n_banded = n_paired = 0
    for gi, (_, dilation) in enumerate(DILATION_GROUPS):
        if _blocks_per_class(dilation) > 1:
            banded_loop(gi, dilation, n_banded)
            n_banded += 2 * HEADS_PER_GROUP
        else:
            paired_loop(gi, dilation, n_paired)
            n_paired += HEADS_PER_GROUP

    rows_per_step = MERGE_ROWS
    for pair in range(n_pairs):
        for c in range(seq // rows_per_step):
            rows = slice(c * rows_per_step, (c + 1) * rows_per_step)
            lses = [lse_s[gi, pair, rows, :] for gi in range(N_ATT_GROUPS)]
            top = functools.reduce(jnp.maximum, lses)
            es = [jnp.exp(l - top) for l in lses]
            den = functools.reduce(lambda a, b: a + b, es)
            num = functools.reduce(
                lambda a, b: a + b, [es[gi] * out_s[gi, pair, rows, :] for gi in range(N_ATT_GROUPS)])
            o_ref[0, rows, pair * LANES_V7X:(pair + 1) * LANES_V7X] = (num / den).astype(BF16)


def _attend(qkv, banded, paired, weights):
    _, batch, seq, width = qkv.shape
    n_pairs = HEADS_PER_GROUP // 2

    def row_slice_spec(w):
        rows = w.shape[-2]
        assert rows % (batch * BF16_SUBLANES_V7X) == 0, "row slices must be whole bf16 tiles"
        lead = w.ndim - 2
        return pl.BlockSpec(w.shape[:-2] + (rows // batch, w.shape[-1]), lambda b: (0,) * lead + (b, 0))

    cast_specs = [row_slice_spec(w) for w in weights]
    results = pl.pallas_call(
        functools.partial(_attend_kernel, seq=seq, n_cast=len(weights)),
        grid=(batch,),
        in_specs=[
            pl.BlockSpec((N_ATT_GROUPS, 1, seq, width), lambda b: (0, b, 0, 0)),
            pl.BlockSpec(banded.shape, lambda b: (0, 0, 0)),
            pl.BlockSpec(paired.shape, lambda b: (0, 0, 0)),
        ] + cast_specs,
        out_specs=[pl.BlockSpec((1, seq, GROUP_WIDTH), lambda b: (b, 0, 0))]
        + [row_slice_spec(w) for w in weights],
        out_shape=[jax.ShapeDtypeStruct((batch, seq, GROUP_WIDTH), BF16)]
        + [jax.ShapeDtypeStruct(w.shape, BF16) for w in weights],
        scratch_shapes=[pltpu.VMEM((N_ATT_GROUPS, n_pairs, seq, LANES_V7X), F32)] * 2,
        compiler_params=pltpu.CompilerParams(
            dimension_semantics=("arbitrary",),
            vmem_limit_bytes=SCOPED_VMEM_LIMIT_BYTES_V7X,
        ),
        name="attend",
    )(qkv, banded, paired, *weights)
    return results[0], results[1:]


def _mixer_kernel(x_ref, xh_ref, mod_ref, att_ref, wp_ref, wpg_ref, ps_ref, wbp_ref, wba_ref, wout_ref,
                  lg_ref, lb_ref, o_ref, u_s, *, tm, n_sub):
    i = pl.program_id(1)
    d = D_MODEL
    sub = tm // n_sub
    shift = mod_ref[0, :, 0:d]
    scale = mod_ref[0, :, d:2 * d]
    gate_c = mod_ref[0, :, 2 * d:3 * d]

    hh = (xh_ref[0] * (1.0 + scale) + shift).astype(BF16)
    pool_cols = slice(3 * ATT_WIDTH, 3 * ATT_WIDTH + d)
    gate_cols = slice(3 * ATT_WIDTH + d, 3 * ATT_WIDTH + 3 * d)
    u_halo = jnp.dot(hh, wp_ref[:, pool_cols], preferred_element_type=F32)
    u_s[0:POOL_HALO, :] = jnp.where(i > 0, u_halo, jnp.zeros_like(u_halo))

    for s in range(n_sub):
        rows = slice(s * sub, (s + 1) * sub)
        base = POOL_HALO + s * sub
        x = x_ref[0, rows, :]
        h = (x * (1.0 + scale) + shift).astype(BF16)
        u_s[base:base + sub, :] = jnp.dot(h, wp_ref[:, pool_cols], preferred_element_type=F32)
        gates = jnp.dot(h, wp_ref[:, gate_cols], preferred_element_type=F32)

        pos = i * tm + s * sub + lax.broadcasted_iota(jnp.int32, (sub, 1), 0)
        pgs = []
        for gi, w in enumerate(POOL_WINDOWS):
            cols = slice(gi * POOL_GROUP_WIDTH, (gi + 1) * POOL_GROUP_WIDTH)
            u_ext = u_s[base - POOL_HALO:base + sub, cols]
            total = u_ext
            reach = 1
            while reach < w:
                total = total + pltpu.roll(total, reach, axis=0)
                reach *= 2
            u = u_ext[POOL_HALO:]
            total = total[POOL_HALO:]
            inv_count = 1.0 / jnp.minimum(pos + 1, w).astype(F32)
            pm = total * inv_count - u
            pg = jnp.dot(pm.astype(BF16), wpg_ref[gi], preferred_element_type=F32) * ps_ref[:, cols]
            pgs.append(pg.astype(BF16))
        branch_b = jnp.dot(jnp.concatenate(pgs, axis=1), wbp_ref[...], preferred_element_type=F32)

        branch_a = jnp.dot(att_ref[0, rows, :], wba_ref[...], preferred_element_type=F32)
        merged = _sigmoid(gates[:, :d]) * branch_a + _sigmoid(gates[:, d:]) * branch_b
        mixer_out = jnp.dot(merged.astype(BF16), wout_ref[...], preferred_element_type=F32)
        o_ref[0, rows, :] = _layer_norm(DEEPNORM_ALPHA * x + gate_c * mixer_out, lg_ref[...], lb_ref[...])


def _const_spec(shape):
    return pl.BlockSpec(shape, lambda b, i: (0,) * len(shape))


def _mixer(x, mod3, att, wp, wpg, pool_scale, wbp, wba, wout, ln_g, ln_b, *, tm, n_sub):
    batch, seq, d = x.shape
    halo_blocks_per_tile = tm // POOL_HALO
    return pl.pallas_call(
        functools.partial(_mixer_kernel, tm=tm, n_sub=n_sub),
        grid=(batch, seq // tm),
        in_specs=[
            pl.BlockSpec((1, tm, d), lambda b, i: (b, i, 0)),
            pl.BlockSpec((1, POOL_HALO, d), lambda b, i: (b, jnp.maximum(i * halo_blocks_per_tile - 1, 0), 0)),
            pl.BlockSpec((1, 1, N_ADA * d), lambda b, i: (b, 0, 0)),
            pl.BlockSpec((1, tm, GROUP_WIDTH), lambda b, i: (b, i, 0)),
            _const_spec(wp.shape), _const_spec(wpg.shape), _const_spec(pool_scale.shape),
            _const_spec(wbp.shape), _const_spec(wba.shape), _const_spec(wout.shape),
            _const_spec(ln_g.shape), _const_spec(ln_b.shape),
        ],
        out_specs=pl.BlockSpec((1, tm, d), lambda b, i: (b, i, 0)),
        out_shape=jax.ShapeDtypeStruct((batch, seq, d), F32),
        scratch_shapes=[pltpu.VMEM((POOL_HALO + tm, d), F32)],
        compiler_params=pltpu.CompilerParams(
            dimension_semantics=("arbitrary", "arbitrary"),
            vmem_limit_bytes=SCOPED_VMEM_LIMIT_BYTES_V7X,
        ),
        name="mixer",
    )(x, x, mod3, att, wp, wpg, pool_scale, wbp, wba, wout, ln_g, ln_b)


def _ffn_kernel(x_ref, mod_ref, wg_ref, wu_ref, wd_ref, lg_ref, lb_ref, o_ref, *, n_sub):
    d = D_MODEL
    shift = mod_ref[0, :, 3 * d:4 * d]
    scale = mod_ref[0, :, 4 * d:5 * d]
    gate_c = mod_ref[0, :, 5 * d:6 * d]
    sub = x_ref.shape[1] // n_sub
    for s in range(n_sub):
        rows = slice(s * sub, (s + 1) * sub)
        x = x_ref[0, rows, :]
        h = (x * (1.0 + scale) + shift).astype(BF16)
        gt = jnp.dot(h, wg_ref[...], preferred_element_type=F32)
        up = jnp.dot(h, wu_ref[...], preferred_element_type=F32)
        act = (gt * _sigmoid(gt) * up).astype(BF16)
        ffn = jnp.dot(act, wd_ref[...], preferred_element_type=F32)
        o_ref[0, rows, :] = _layer_norm(DEEPNORM_ALPHA * x + gate_c * ffn, lg_ref[...], lb_ref[...])


def _ffn(x1, mod3, wg, wu, wd, ln_g, ln_b, *, tm, n_sub):
    batch, seq, d = x1.shape
    return pl.pallas_call(
        functools.partial(_ffn_kernel, n_sub=n_sub),
        grid=(batch, seq // tm),
        in_specs=[
            pl.BlockSpec((1, tm, d), lambda b, i: (b, i, 0)),
            pl.BlockSpec((1, 1, N_ADA * d), lambda b, i: (b, 0, 0)),
            _const_spec(wg.shape), _const_spec(wu.shape), _const_spec(wd.shape),
            _const_spec(ln_g.shape), _const_spec(ln_b.shape),
        ],
        out_specs=pl.BlockSpec((1, tm, d), lambda b, i: (b, i, 0)),
        out_shape=jax.ShapeDtypeStruct((batch, seq, d), F32),
        compiler_params=pltpu.CompilerParams(
            dimension_semantics=("arbitrary", "arbitrary"),
            vmem_limit_bytes=SCOPED_VMEM_LIMIT_BYTES_V7X,
        ),
        name="ffn",
    )(x1, mod3, wg, wu, wd, ln_g, ln_b)


def kernel(x, c, w_ada, b_ada, w_in, w_branch_att, w_pool_group, pool_scale, w_branch_pool, w_out, ln1_g, ln1_b,
           w_gate, w_up, w_down, ln2_g, ln2_b):
    batch, seq, d = x.shape
    assert d == D_MODEL and w_ada.shape[0] == 1, "one layer of width D_MODEL"
    assert seq == BLOCK * DILATION_GROUPS[-1][1], "sequence must be one block per residue class of the widest dilation"
    l = 0
    mod, w_in_bf16 = _ada(c, w_ada[l], b_ada[l], w_in[l])
    mod3 = mod.reshape(batch, 1, N_ADA * d)

    qkv = _inproj(x, mod3, w_in_bf16)
    banded, paired = _bias_tables()
    att, (wpg, wbp, wba, wout, wg, wu, wd) = _attend(
        qkv, jnp.asarray(banded), jnp.asarray(paired),
        [w_pool_group[l], w_branch_pool[l], w_branch_att[l], w_out[l], w_gate[l], w_up[l], w_down[l]])

    x1 = _mixer(
        x, mod3, att, w_in_bf16, wpg, pool_scale[l].reshape(1, d), wbp, wba, wout,
        ln1_g[l].reshape(1, d), ln1_b[l].reshape(1, d), tm=MIXER_TILE_ROWS, n_sub=MIXER_SUBTILES)
    return _ffn(x1, mod3, wg, wu, wd,
                ln2_g[l].reshape(1, d), ln2_b[l].reshape(1, d), tm=FFN_TILE_ROWS, n_sub=FFN_SUBTILES)
```

```python
import functools
import math

import jax
import jax.numpy as jnp
import numpy as np
from jax import lax
from jax.experimental import pallas as pl
from jax.experimental.pallas import tpu as pltpu

D_MODEL = 1024
HEAD_DIM = 64
HEADS_PER_GROUP = 4
DILATION_GROUPS = ((128, 1), (512, 4), (2048, 16))
N_ATT_GROUPS = len(DILATION_GROUPS)
N_ATT_HEADS = N_ATT_GROUPS * HEADS_PER_GROUP
ATT_WIDTH = N_ATT_HEADS * HEAD_DIM
GROUP_WIDTH = HEADS_PER_GROUP * HEAD_DIM
BLOCK = 128
POOL_WINDOWS = (2, 4, 8, 16)
POOL_GROUP_WIDTH = D_MODEL // len(POOL_WINDOWS)
POOL_HALO = 16
N_ADA = 6
DEEPNORM_ALPHA = 2.0 ** 0.25
LN_EPS = 1e-5

LANES_V7X = 128
BF16_SUBLANES_V7X = 16
SCOPED_VMEM_LIMIT_BYTES_V7X = 60000 * 1024

INPROJ_ROW_CHUNK = 512
MERGE_ROWS = 256
MIXER_TILE_ROWS, MIXER_SUBTILES = 1024, 4
FFN_TILE_ROWS, FFN_SUBTILES = 1024, 8
FFN_HIDDEN_SPLITS = (0, 1536, 2816)

BF16 = jnp.bfloat16
F32 = jnp.float32


def _alibi_slopes(n):
    def pow2_slopes(m):
        start = 2.0 ** (-8.0 / m)
        return [start ** (i + 1) for i in range(m)]
    if math.log2(n).is_integer():
        s = pow2_slopes(n)
    else:
        c = 2 ** math.floor(math.log2(n))
        s = pow2_slopes(c) + pow2_slopes(2 * c)[0::2][: n - c]
    return np.array(sorted(s, reverse=True), dtype=np.float32)


def _bias_tables():
    slopes = _alibi_slopes(N_ATT_HEADS).reshape(N_ATT_GROUPS, HEADS_PER_GROUP)
    qi = np.arange(BLOCK) + BLOCK
    kj = np.arange(2 * BLOCK)
    diff = qi[:, None] - kj[None, :]
    banded, paired = [], []
    for g, (window, dilation) in enumerate(DILATION_GROUPS):
        steps = window // dilation
        valid = (diff >= 0) & (diff <= steps)
        dist = (diff * dilation).astype(np.float32)
        for h in range(HEADS_PER_GROUP):
            bias = -slopes[g, h] * dist
            normal = np.where(valid, bias, -np.inf).astype(np.float32)
            first = np.where(valid & (kj[None, :] >= BLOCK), bias, -np.inf).astype(np.float32)
            if _blocks_per_class(dilation) > 1:
                banded += [normal, first]
            else:
                own = normal[:, BLOCK:]
                off = np.full_like(own, -np.inf)
                paired.append(np.block([[own, off], [off, own]]))
    return np.stack(banded), np.stack(paired)


def _blocks_per_class(dilation):
    return DILATION_GROUPS[-1][1] // dilation


def _layer_norm(y, g, b):
    mu = jnp.mean(y, axis=-1, keepdims=True)
    yc = y - mu
    var = jnp.mean(yc * yc, axis=-1, keepdims=True)
    return yc * lax.rsqrt(var + LN_EPS) * g + b


def _sigmoid(x):
    return 1.0 / (1.0 + jnp.exp(-x))


def _ada_kernel(c_ref, w_ref, b_ref, win_ref, o_ref, win_bf16_ref):
    c = c_ref[...]
    s = (c * _sigmoid(c)).astype(BF16)
    o_ref[...] = jnp.dot(s, w_ref[...].astype(BF16), preferred_element_type=F32) + b_ref[...]
    win_bf16_ref[...] = win_ref[...].astype(BF16)


def _ada(c, w_ada, b_ada, w_in):
    batch, d = c.shape
    n = w_ada.shape[1]
    tn = d
    steps = n // tn
    win_cols = w_in.shape[1] // steps
    assert win_cols * steps == w_in.shape[1] and win_cols % LANES_V7X == 0
    return pl.pallas_call(
        _ada_kernel,
        grid=(steps,),
        in_specs=[
            pl.BlockSpec((batch, d), lambda j: (0, 0)),
            pl.BlockSpec((d, tn), lambda j: (0, j)),
            pl.BlockSpec((1, tn), lambda j: (0, j)),
            pl.BlockSpec((w_in.shape[0], win_cols), lambda j: (0, j)),
        ],
        out_specs=[
            pl.BlockSpec((batch, tn), lambda j: (0, j)),
            pl.BlockSpec((w_in.shape[0], win_cols), lambda j: (0, j)),
        ],
        out_shape=[jax.ShapeDtypeStruct((batch, n), F32), jax.ShapeDtypeStruct(w_in.shape, BF16)],
        name="ada",
    )(c, w_ada, b_ada.reshape(1, n), w_in)


def _inproj_kernel(*refs, seq, row_chunk):
    n_slab = D_MODEL // LANES_V7X
    x_refs = refs[:n_slab]
    mod_ref, w_ref, o_ref, h_ref, hf_ref = refs[n_slab:]
    n_blocks = seq // BLOCK
    blocks_per_chunk = row_chunk // BLOCK
    n_slots = h_ref.shape[0]
    gi_kept = N_ATT_GROUPS - 2
    prev_dilation = DILATION_GROUPS[gi_kept][1]
    ratio = DILATION_GROUPS[-1][1] // prev_dilation

    def modulate_rows(gi, chunk, slot):
        dilation = DILATION_GROUPS[gi][1]
        blocks_per_class = n_blocks // dilation
        for kk in range(blocks_per_chunk):
            k = chunk * blocks_per_chunk + kk
            r, n = divmod(k, blocks_per_class)
            src = r + n * (BLOCK * dilation)
            for j in range(n_slab):
                lanes = slice(j * LANES_V7X, (j + 1) * LANES_V7X)
                if gi == N_ATT_GROUPS - 1:
                    c, r_prev = divmod(r, prev_dilation)
                    start = r_prev * (seq // prev_dilation) + n * (BLOCK * ratio) + c
                    hv = hf_ref[j, pl.ds(start, BLOCK, stride=ratio), :]
                else:
                    if dilation == 1:
                        xs = x_refs[j][0, src:src + BLOCK, :]
                    else:
                        xs = x_refs[j][0, pl.ds(src, BLOCK, stride=dilation), :]
                    shift = mod_ref[0, :, lanes]
                    scale = mod_ref[0, :, D_MODEL + j * LANES_V7X:D_MODEL + (j + 1) * LANES_V7X]
                    hv = xs * (1.0 + scale) + shift
                    if gi == gi_kept:
                        hf_ref[j, k * BLOCK:(k + 1) * BLOCK, :] = hv
                h_ref[slot, kk * BLOCK:(kk + 1) * BLOCK, lanes] = hv.astype(BF16)

    def project_rows(gi, chunk, slot):
        rows = slice(chunk * row_chunk, (chunk + 1) * row_chunk)
        for part in range(3):
            c0 = part * ATT_WIDTH + gi * GROUP_WIDTH
            proj = jnp.dot(h_ref[slot], w_ref[:, c0:c0 + GROUP_WIDTH], preferred_element_type=F32)
            if part == 0:
                proj = proj * (1.0 / math.sqrt(HEAD_DIM))
            o_ref[gi, 0, rows, part * GROUP_WIDTH:(part + 1) * GROUP_WIDTH] = proj.astype(BF16)

    items = [(gi, chunk) for gi in range(N_ATT_GROUPS) for chunk in range(seq // row_chunk)]
    modulate_rows(*items[0], 0)
    for n, item in enumerate(items):
        if n + 1 < len(items):
            modulate_rows(*items[n + 1], (n + 1) % n_slots)
        project_rows(*item, n % n_slots)


def _inproj(x, mod3, w_in):
    batch, seq, d = x.shape
    n_slab = d // LANES_V7X
    row_chunk = INPROJ_ROW_CHUNK
    x_specs = [
        pl.BlockSpec((1, seq, LANES_V7X), functools.partial(lambda b, j: (b, 0, j), j=j))
        for j in range(n_slab)
    ]
    return pl.pallas_call(
        functools.partial(_inproj_kernel, seq=seq, row_chunk=row_chunk),
        grid=(batch,),
        in_specs=x_specs + [
            pl.BlockSpec((1, 1, N_ADA * d), lambda b: (b, 0, 0)),
            pl.BlockSpec((d, 3 * ATT_WIDTH), lambda b: (0, 0), pipeline_mode=pl.Buffered(1)),
        ],
        out_specs=pl.BlockSpec((N_ATT_GROUPS, 1, seq, 3 * GROUP_WIDTH), lambda b: (0, b, 0, 0)),
        out_shape=jax.ShapeDtypeStruct((N_ATT_GROUPS, batch, seq, 3 * GROUP_WIDTH), BF16),
        scratch_shapes=[pltpu.VMEM((2, row_chunk, d), BF16), pltpu.VMEM((n_slab, seq, LANES_V7X), F32)],
        compiler_params=pltpu.CompilerParams(
            dimension_semantics=("arbitrary",),
            vmem_limit_bytes=SCOPED_VMEM_LIMIT_BYTES_V7X,
        ),
        name="inproj",
    )(*([x] * n_slab), mod3, w_in)


def _attend_kernel(*refs, seq, n_cast):
    qkv_ref, banded_ref, paired_ref = refs[:3]
    cast_src_refs = refs[3:3 + n_cast]
    o_ref = refs[3 + n_cast]
    cast_dst_refs = refs[4 + n_cast:4 + 2 * n_cast]
    out_s, lse_s = refs[4 + 2 * n_cast:]
    stat_refs = (out_s, lse_s)

    for src, dst in zip(cast_src_refs, cast_dst_refs):
        dst[...] = src[...].astype(BF16)

    n_blocks = seq // BLOCK
    n_pairs = HEADS_PER_GROUP // 2
    ones = jnp.ones((2 * BLOCK, LANES_V7X), BF16)

    def softmax_pv(q2, k2, v2e, load_bias):
        rows = q2.shape[0]
        low_half = lax.broadcasted_iota(jnp.int32, (rows, LANES_V7X), 1) < HEAD_DIM
        accs, dens, maxes = [], [], []
        for hh in range(2):
            keep = low_half if hh == 0 else jnp.logical_not(low_half)
            qm = jnp.where(keep, q2, jnp.zeros_like(q2))
            s = lax.dot_general(qm, k2, (((1,), (1,)), ((), ())), preferred_element_type=F32)
            s = s + load_bias(hh)
            m = jnp.max(s, axis=-1, keepdims=True)
            p = jnp.exp(s - m).astype(BF16)
            o = jnp.dot(p, v2e, preferred_element_type=F32)
            accs.append(o[:, :LANES_V7X])
            dens.append(o[:, LANES_V7X:])
            maxes.append(jnp.broadcast_to(m, (rows, LANES_V7X)))
        acc = jnp.where(low_half, accs[0], accs[1])
        den = jnp.where(low_half, dens[0], dens[1])
        mx = jnp.where(low_half, maxes[0], maxes[1])
        return acc / den, mx + jnp.log(den)

    def put(gi, pair, idx, stats, rows=slice(None)):
        for ref, value in zip(stat_refs, stats):
            ref[gi, pair, idx, :] = value[rows]

    def cols(part, pair):
        c0 = part * GROUP_WIDTH + pair * LANES_V7X
        return slice(c0, c0 + LANES_V7X)

    def banded_loop(gi, dilation, table0):
        blocks_per_class = n_blocks // dilation

        def body(j, carry):
            row0 = pl.multiple_of(j * BLOCK, BLOCK)
            n = j % blocks_per_class
            r = j // blocks_per_class
            first = jnp.where(n == 0, 1, 0)
            prow0 = pl.multiple_of(jnp.maximum(j - 1, 0) * BLOCK, BLOCK)
            nat0 = r + n * (BLOCK * dilation)
            for pair in range(n_pairs):
                q2 = qkv_ref[gi, 0,pl.ds(row0, BLOCK), cols(0, pair)]
                k2 = jnp.concatenate([qkv_ref[gi, 0,pl.ds(prow0, BLOCK), cols(1, pair)],
                                      qkv_ref[gi, 0,pl.ds(row0, BLOCK), cols(1, pair)]], axis=0)
                v2 = jnp.concatenate([qkv_ref[gi, 0,pl.ds(prow0, BLOCK), cols(2, pair)],
                                      qkv_ref[gi, 0,pl.ds(row0, BLOCK), cols(2, pair)]], axis=0)
                v2e = jnp.concatenate([v2, ones], axis=1)
                stats = softmax_pv(
                    q2, k2, v2e, lambda hh: banded_ref[table0 + (pair * 2 + hh) * 2 + first])
                if dilation == 1:
                    put(gi, pair, pl.ds(row0, BLOCK), stats)
                else:
                    put(gi, pair, pl.ds(nat0, BLOCK, stride=dilation), stats)
            return carry

        lax.fori_loop(0, n_blocks, body, 0, unroll=16)

    def paired_loop(gi, dilation, table0):
        def body(j, carry):
            row0 = pl.multiple_of(j * (2 * BLOCK), 2 * BLOCK)
            for pair in range(n_pairs):
                q2 = qkv_ref[gi, 0,pl.ds(row0, 2 * BLOCK), cols(0, pair)]
                k2 = qkv_ref[gi, 0,pl.ds(row0, 2 * BLOCK), cols(1, pair)]
                v2e = jnp.concatenate([qkv_ref[gi, 0,pl.ds(row0, 2 * BLOCK), cols(2, pair)], ones], axis=1)
                stats = softmax_pv(q2, k2, v2e, lambda hh: paired_ref[table0 + pair * 2 + hh])
                for half in range(2):
                    put(gi, pair, pl.ds(2 * j + half, BLOCK, stride=dilation), stats,
                        rows=slice(half * BLOCK, (half + 1) * BLOCK))
            return carry

        lax.fori_loop(0, n_blocks // 2, body, 0, unroll=8)

    n_banded = n_paired = 0
    for gi, (_, dilation) in enumerate(DILATION_GROUPS):
        if _blocks_per_class(dilation) > 1:
            banded_loop(gi, dilation, n_banded)
            n_banded += 2 * HEADS_PER_GROUP
        else:
            paired_loop(gi, dilation, n_paired)
            n_paired += HEADS_PER_GROUP

    rows_per_step = MERGE_ROWS
    for pair in range(n_pairs):
        for c in range(seq // rows_per_step):
            rows = slice(c * rows_per_step, (c + 1) * rows_per_step)
            lses = [lse_s[gi, pair, rows, :] for gi in range(N_ATT_GROUPS)]
            top = functools.reduce(jnp.maximum, lses)
            es = [jnp.exp(l - top) for l in lses]
            den = functools.reduce(lambda a, b: a + b, es)
            num = functools.reduce(
                lambda a, b: a + b, [es[gi] * out_s[gi, pair, rows, :] for gi in range(N_ATT_GROUPS)])
            o_ref[0, rows, pair * LANES_V7X:(pair + 1) * LANES_V7X] = (num / den).astype(BF16)


def _attend(qkv, banded, paired, weights):
    _, batch, seq, width = qkv.shape
    n_pairs = HEADS_PER_GROUP // 2

    def row_slice_spec(w):
        rows = w.shape[-2]
        assert rows % (batch * BF16_SUBLANES_V7X) == 0, "row slices must be whole bf16 tiles"
        lead = w.ndim - 2
        return pl.BlockSpec(w.shape[:-2] + (rows // batch, w.shape[-1]), lambda b: (0,) * lead + (b, 0))

    cast_specs = [row_slice_spec(w) for w in weights]
    results = pl.pallas_call(
        functools.partial(_attend_kernel, seq=seq, n_cast=len(weights)),
        grid=(batch,),
        in_specs=[
            pl.BlockSpec((N_ATT_GROUPS, 1, seq, width), lambda b: (0, b, 0, 0)),
            pl.BlockSpec(banded.shape, lambda b: (0, 0, 0)),
            pl.BlockSpec(paired.shape, lambda b: (0, 0, 0)),
        ] + cast_specs,
        out_specs=[pl.BlockSpec((1, seq, GROUP_WIDTH), lambda b: (b, 0, 0))]
        + [row_slice_spec(w) for w in weights],
        out_shape=[jax.ShapeDtypeStruct((batch, seq, GROUP_WIDTH), BF16)]
        + [jax.ShapeDtypeStruct(w.shape, BF16) for w in weights],
        scratch_shapes=[pltpu.VMEM((N_ATT_GROUPS, n_pairs, seq, LANES_V7X), F32)] * 2,
        compiler_params=pltpu.CompilerParams(
            dimension_semantics=("arbitrary",),
            vmem_limit_bytes=SCOPED_VMEM_LIMIT_BYTES_V7X,
        ),
        name="attend",
    )(qkv, banded, paired, *weights)
    return results[0], results[1:]


def _mixer_kernel(x_ref, xh_ref, mod_ref, att_ref, wp_ref, wpg_ref, ps_ref, wbp_ref, wba_ref, wout_ref,
                  lg_ref, lb_ref, o_ref, u_s, *, tm, n_sub):
    i = pl.program_id(1)
    d = D_MODEL
    sub = tm // n_sub
    shift = mod_ref[0, :, 0:d]
    scale = mod_ref[0, :, d:2 * d]
    gate_c = mod_ref[0, :, 2 * d:3 * d]

    hh = (xh_ref[0] * (1.0 + scale) + shift).astype(BF16)
    pool_cols = slice(3 * ATT_WIDTH, 3 * ATT_WIDTH + d)
    gate_cols = slice(3 * ATT_WIDTH + d, 3 * ATT_WIDTH + 3 * d)
    u_halo = jnp.dot(hh, wp_ref[:, pool_cols], preferred_element_type=F32)
    u_s[0:POOL_HALO, :] = jnp.where(i > 0, u_halo, jnp.zeros_like(u_halo))

    for s in range(n_sub):
        rows = slice(s * sub, (s + 1) * sub)
        base = POOL_HALO + s * sub
        x = x_ref[0, rows, :]
        h = (x * (1.0 + scale) + shift).astype(BF16)
        u_s[base:base + sub, :] = jnp.dot(h, wp_ref[:, pool_cols], preferred_element_type=F32)
        gates = jnp.dot(h, wp_ref[:, gate_cols], preferred_element_type=F32)

        pos = i * tm + s * sub + lax.broadcasted_iota(jnp.int32, (sub, 1), 0)
        pgs = []
        for gi, w in enumerate(POOL_WINDOWS):
            cols = slice(gi * POOL_GROUP_WIDTH, (gi + 1) * POOL_GROUP_WIDTH)
            u_ext = u_s[base - POOL_HALO:base + sub, cols]
            total = u_ext
            reach = 1
            while reach < w:
                total = total + pltpu.roll(total, reach, axis=0)
                reach *= 2
            u = u_ext[POOL_HALO:]
            total = total[POOL_HALO:]
            inv_count = 1.0 / jnp.minimum(pos + 1, w).astype(F32)
            pm = total * inv_count - u
            pg = jnp.dot(pm.astype(BF16), wpg_ref[gi], preferred_element_type=F32) * ps_ref[:, cols]
            pgs.append(pg.astype(BF16))
        branch_b = jnp.dot(jnp.concatenate(pgs, axis=1), wbp_ref[...], preferred_element_type=F32)

        branch_a = jnp.dot(att_ref[0, rows, :], wba_ref[...], preferred_element_type=F32)
        merged = _sigmoid(gates[:, :d]) * branch_a + _sigmoid(gates[:, d:]) * branch_b
        mixer_out = jnp.dot(merged.astype(BF16), wout_ref[...], preferred_element_type=F32)
        o_ref[0, rows, :] = _layer_norm(DEEPNORM_ALPHA * x + gate_c * mixer_out, lg_ref[...], lb_ref[...])


def _const_spec(shape):
    return pl.BlockSpec(shape, lambda b, i: (0,) * len(shape))


def _mixer(x, mod3, att, wp, wpg, pool_scale, wbp, wba, wout, ln_g, ln_b, *, tm, n_sub):
    batch, seq, d = x.shape
    halo_blocks_per_tile = tm // POOL_HALO
    return pl.pallas_call(
        functools.partial(_mixer_kernel, tm=tm, n_sub=n_sub),
        grid=(batch, seq // tm),
        in_specs=[
            pl.BlockSpec((1, tm, d), lambda b, i: (b, i, 0)),
            pl.BlockSpec((1, POOL_HALO, d), lambda b, i: (b, jnp.maximum(i * halo_blocks_per_tile - 1, 0), 0)),
            pl.BlockSpec((1, 1, N_ADA * d), lambda b, i: (b, 0, 0)),
            pl.BlockSpec((1, tm, GROUP_WIDTH), lambda b, i: (b, i, 0)),
            _const_spec(wp.shape), _const_spec(wpg.shape), _const_spec(pool_scale.shape),
            _const_spec(wbp.shape), _const_spec(wba.shape), _const_spec(wout.shape),
            _const_spec(ln_g.shape), _const_spec(ln_b.shape),
        ],
        out_specs=pl.BlockSpec((1, tm, d), lambda b, i: (b, i, 0)),
        out_shape=jax.ShapeDtypeStruct((batch, seq, d), F32),
        scratch_shapes=[pltpu.VMEM((POOL_HALO + tm, d), F32)],
        compiler_params=pltpu.CompilerParams(
            dimension_semantics=("arbitrary", "arbitrary"),
            vmem_limit_bytes=SCOPED_VMEM_LIMIT_BYTES_V7X,
        ),
        name="mixer",
    )(x, x, mod3, att, wp, wpg, pool_scale, wbp, wba, wout, ln_g, ln_b)


def _ffn_kernel(x_ref, mod_ref, wg_ref, wu_ref, wd_ref, lg_ref, lb_ref, o_ref, *, n_sub):
    d = D_MODEL
    shift = mod_ref[0, :, 3 * d:4 * d]
    scale = mod_ref[0, :, 4 * d:5 * d]
    gate_c = mod_ref[0, :, 5 * d:6 * d]
    sub = x_ref.shape[1] // n_sub
    for s in range(n_sub):
        rows = slice(s * sub, (s + 1) * sub)
        x = x_ref[0, rows, :]
        h = (x * (1.0 + scale) + shift).astype(BF16)
        ffn = None
        for lo, hi in zip(FFN_HIDDEN_SPLITS[:-1], FFN_HIDDEN_SPLITS[1:]):
            cols = slice(lo, hi)
            gt = jnp.dot(h, wg_ref[:, cols], preferred_element_type=F32)
            up = jnp.dot(h, wu_ref[:, cols], preferred_element_type=F32)
            act = (gt * _sigmoid(gt) * up).astype(BF16)
            part = jnp.dot(act, wd_ref[cols, :], preferred_element_type=F32)
            ffn = part if ffn is None else ffn + part
        o_ref[0, rows, :] = _layer_norm(DEEPNORM_ALPHA * x + gate_c * ffn, lg_ref[...], lb_ref[...])


def _ffn(x1, mod3, wg, wu, wd, ln_g, ln_b, *, tm, n_sub):
    batch, seq, d = x1.shape
    return pl.pallas_call(
        functools.partial(_ffn_kernel, n_sub=n_sub),
        grid=(batch, seq // tm),
        in_specs=[
            pl.BlockSpec((1, tm, d), lambda b, i: (b, i, 0)),
            pl.BlockSpec((1, 1, N_ADA * d), lambda b, i: (b, 0, 0)),
            _const_spec(wg.shape), _const_spec(wu.shape), _const_spec(wd.shape),
            _const_spec(ln_g.shape), _const_spec(ln_b.shape),
        ],
        out_specs=pl.BlockSpec((1, tm, d), lambda b, i: (b, i, 0)),
        out_shape=jax.ShapeDtypeStruct((batch, seq, d), F32),
        compiler_params=pltpu.CompilerParams(
            dimension_semantics=("arbitrary", "arbitrary"),
            vmem_limit_bytes=SCOPED_VMEM_LIMIT_BYTES_V7X,
        ),
        name="ffn",
    )(x1, mod3, wg, wu, wd, ln_g, ln_b)


def kernel(x, c, w_ada, b_ada, w_in, w_branch_att, w_pool_group, pool_scale, w_branch_pool, w_out, ln1_g, ln1_b,
           w_gate, w_up, w_down, ln2_g, ln2_b):
    batch, seq, d = x.shape
    assert d == D_MODEL and w_ada.shape[0] == 1, "one layer of width D_MODEL"
    assert seq == BLOCK * DILATION_GROUPS[-1][1], "sequence must be one block per residue class of the widest dilation"
    l = 0
    mod, w_in_bf16 = _ada(c, w_ada[l], b_ada[l], w_in[l])
    mod3 = mod.reshape(batch, 1, N_ADA * d)

    qkv = _inproj(x, mod3, w_in_bf16)
    banded, paired = _bias_tables()
    att, (wpg, wbp, wba, wout, wg, wu, wd) = _attend(
        qkv, jnp.asarray(banded), jnp.asarray(paired),
        [w_pool_group[l], w_branch_pool[l], w_branch_att[l], w_out[l], w_gate[l], w_up[l], w_down[l]])

    x1 = _mixer(
        x, mod3, att, w_in_bf16, wpg, pool_scale[l].reshape(1, d), wbp, wba, wout,
        ln1_g[l].reshape(1, d), ln1_b[l].reshape(1, d), tm=MIXER_TILE_ROWS, n_sub=MIXER_SUBTILES)
    return _ffn(x1, mod3, wg, wu, wd,
                ln2_g[l].reshape(1, d), ln2_b[l].reshape(1, d), tm=FFN_TILE_ROWS, n_sub=FFN_SUBTILES)
```
